```python
import jax, jax.numpy as jnp
from jax import lax
import numpy as np

D_MODEL = 1024
BATCH = 8
SEQ = 4096
DEPTH = 2
DEC_BATCH = 32
DEC_SEQ = 64
PAST_LEN = 1024

CHUNK = 64
GMLP_CHUNK = 128
GMLP_GROUPS = 8
D_V = 3 * D_MODEL
GMLP_GROUP_DIM = D_V // GMLP_GROUPS
CONV_WIDTH = 31
CONV_STATE = CONV_WIDTH - 1
D_FF = 7 * D_MODEL // 2
N_EXPERTS = 8
TOP_K = 2
N_GMLP_LAYERS = (DEPTH + 1) // 2
N_CONV_LAYERS = DEPTH // 2
LN_EPS = 1e-5
DEEPNORM_ALPHA = (2.0 * DEPTH) ** 0.25
DEEPNORM_BETA = (8.0 * DEPTH) ** -0.25

kernel_name = 'streaming_gmlp_conformer_hybrid_step'


def layer_norm(x, g, b):
    xf = x.astype(jnp.float32)
    mu = jnp.mean(xf, axis=-1, keepdims=True)
    xc = xf - mu
    var = jnp.mean(xc * xc, axis=-1, keepdims=True)
    y = xc * lax.rsqrt(var + LN_EPS) * g.astype(jnp.float32) + b.astype(jnp.float32)
    return y.astype(x.dtype)


def gmlp_mixer(x, w_in, b_in, lnv_g, lnv_b, w_s, b_s, w_out, b_out, block, offset):
    bsz, seq, _ = x.shape
    h = jax.nn.gelu(x @ w_in + b_in, approximate=False)
    u, v = jnp.split(h, 2, axis=-1)
    v = layer_norm(v, lnv_g, lnv_b)
    pos = np.arange(offset, offset + block) // CHUNK
    mask = pos[None, :] <= pos[:, None]
    w = jnp.where(mask, w_s[:, offset:offset + block, offset:offset + block], 0)
    vb = v.reshape(bsz, seq // block, block, GMLP_GROUPS, GMLP_GROUP_DIM)
    bias = jnp.transpose(b_s[:, offset:offset + block])[:, :, None]
    mixed = jnp.einsum('gij,bnjgc->bnigc', w, vb) + bias
    out = u * mixed.reshape(bsz, seq, D_V)
    return out @ w_out + b_out, v


def conv_glu(x, w_pw1, b_pw1):
    a, gate = jnp.split(x @ w_pw1 + b_pw1, 2, axis=-1)
    return a * jax.nn.sigmoid(gate)


def conv_tail(rows, w_dw, b_dw, ln_g, ln_b, w_pw2, b_pw2):
    y = lax.conv_general_dilated(
        rows, w_dw[:, None, :], window_strides=(1,), padding='VALID',
        dimension_numbers=('NWC', 'WIO', 'NWC'), feature_group_count=D_MODEL) + b_dw
    y = jax.nn.silu(layer_norm(y, ln_g, ln_b))
    return y @ w_pw2 + b_pw2


def swiglu(x, wg, wu, wd):
    return (jax.nn.silu(x @ wg) * (x @ wu)) @ wd


def moe_swiglu(x, w_router, b_router, wg, wu, wd):
    shape = x.shape
    t = x.reshape(-1, shape[-1])
    logits = (t @ w_router).astype(jnp.float32) + b_router.astype(jnp.float32)
    top_val, top_idx = lax.top_k(logits, TOP_K)
    top_w = jax.nn.softmax(top_val, axis=-1)
    gates = jnp.einsum('tk,tke->te', top_w,
                       jax.nn.one_hot(top_idx, N_EXPERTS, dtype=jnp.float32)).astype(t.dtype)
    out = jnp.zeros_like(t)
    for e in range(N_EXPERTS):
        out = out + gates[:, e:e + 1] * swiglu(t, wg[e], wu[e], wd[e])
    return out.reshape(shape)


def setup_inputs(seed: int = 0) -> dict:
    key = jax.random.key(seed)
    ks = iter(jax.random.split(key, 40))

    def nrm(shape, scale):
        return jax.random.normal(next(ks), shape, jnp.float32) * scale

    def gain(shape):
        return 1.0 + nrm(shape, 0.01)

    nA, nC, D, F, E = N_GMLP_LAYERS, N_CONV_LAYERS, D_MODEL, D_FF, N_EXPERTS
    return {
        'x_prompt': nrm((BATCH, SEQ, D), 1.0),
        'x_sample': nrm((DEC_BATCH, DEC_SEQ, D), 1.0),
        'cache_conv': nrm((nC, DEC_BATCH, CONV_STATE, D), 0.5),
        'gm_w_in': nrm((nA, D, 2 * D_V), D ** -0.5),
        'gm_b_in': nrm((nA, 2 * D_V), 0.01),
        'gm_lnv_g': gain((nA, D_V)),
        'gm_lnv_b': nrm((nA, D_V), 0.01),
        'gm_w_s': nrm((nA, GMLP_GROUPS, GMLP_CHUNK, GMLP_CHUNK), GMLP_CHUNK ** -0.5),
        'gm_b_s': gain((nA, GMLP_GROUPS, GMLP_CHUNK)),
        'gm_w_out': nrm((nA, D_V, D), D_V ** -0.5 * DEEPNORM_BETA),
        'gm_b_out': nrm((nA, D), 0.01),
        'cv_w_pw1': nrm((nC, D, 2 * D), D ** -0.5),
        'cv_b_pw1': nrm((nC, 2 * D), 0.01),
        'cv_w_dw': nrm((nC, CONV_WIDTH, D), CONV_WIDTH ** -0.5),
        'cv_b_dw': nrm((nC, D), 0.01),
        'cv_ln_g': gain((nC, D)),
        'cv_ln_b': nrm((nC, D), 0.01),
        'cv_w_pw2': nrm((nC, D, D), D ** -0.5 * DEEPNORM_BETA),
        'cv_b_pw2': nrm((nC, D), 0.01),
        'ff_w_gate': nrm((nA, D, F), D ** -0.5),
        'ff_w_up': nrm((nA, D, F), D ** -0.5),
        'ff_w_down': nrm((nA, F, D), F ** -0.5 * DEEPNORM_BETA),
        'moe_w_router': nrm((nC, D, E), D ** -0.5),
        'moe_b_router': nrm((nC, E), 0.01),
        'moe_w_gate': nrm((nC, E, D, F), D ** -0.5),
        'moe_w_up': nrm((nC, E, D, F), D ** -0.5),
        'moe_w_down': nrm((nC, E, F, D), F ** -0.5 * DEEPNORM_BETA),
        'ln_g': gain((DEPTH, 2, D)),
        'ln_b': nrm((DEPTH, 2, D), 0.01),
    }


def reference(x_prompt, x_sample, cache_conv,
              gm_w_in, gm_b_in, gm_lnv_g, gm_lnv_b, gm_w_s, gm_b_s, gm_w_out, gm_b_out,
              cv_w_pw1, cv_b_pw1, cv_w_dw, cv_b_dw, cv_ln_g, cv_ln_b, cv_w_pw2, cv_b_pw2,
              ff_w_gate, ff_w_up, ff_w_down,
              moe_w_router, moe_b_router, moe_w_gate, moe_w_up, moe_w_down,
              ln_g, ln_b):
    xp, xs = x_prompt, x_sample
    gmlp_offset = PAST_LEN % GMLP_CHUNK
    gm_state_p, gm_state_s, conv_state_p, conv_state_s = [], [], [], []
    for i in range(DEPTH):
        j = i // 2
        if i % 2 == 0:
            gm = (gm_w_in[j], gm_b_in[j], gm_lnv_g[j], gm_lnv_b[j], gm_w_s[j], gm_b_s[j],
                  gm_w_out[j], gm_b_out[j])
            mp, vp = gmlp_mixer(xp, *gm, GMLP_CHUNK, 0)
            ms, vs = gmlp_mixer(xs, *gm, xs.shape[1], gmlp_offset)
            gm_state_p.append(vp[:, -GMLP_CHUNK:])
            gm_state_s.append(vs)
        else:
            gp = conv_glu(xp, cv_w_pw1[j], cv_b_pw1[j])
            gs = conv_glu(xs, cv_w_pw1[j], cv_b_pw1[j])
            rows_p = jnp.concatenate(
                [jnp.zeros((gp.shape[0], CONV_STATE, D_MODEL), gp.dtype), gp], axis=1)
            rows_s = jnp.concatenate([cache_conv[j].astype(gs.dtype), gs], axis=1)
            tail = (cv_w_dw[j], cv_b_dw[j], cv_ln_g[j], cv_ln_b[j], cv_w_pw2[j], cv_b_pw2[j])
            mp = conv_tail(rows_p, *tail)
            ms = conv_tail(rows_s, *tail)
            conv_state_p.append(rows_p[:, -CONV_STATE:])
            conv_state_s.append(rows_s[:, -CONV_STATE:])
        xp = layer_norm(DEEPNORM_ALPHA * xp + mp, ln_g[i, 0], ln_b[i, 0])
        xs = layer_norm(DEEPNORM_ALPHA * xs + ms, ln_g[i, 0], ln_b[i, 0])
        if i % 2 == 0:
            fp = swiglu(xp, ff_w_gate[j], ff_w_up[j], ff_w_down[j])
            fs = swiglu(xs, ff_w_gate[j], ff_w_up[j], ff_w_down[j])
        else:
            moe = (moe_w_router[j], moe_b_router[j], moe_w_gate[j], moe_w_up[j], moe_w_down[j])
            fp = moe_swiglu(xp, *moe)
            fs = moe_swiglu(xs, *moe)
        xp = layer_norm(DEEPNORM_ALPHA * xp + fp, ln_g[i, 1], ln_b[i, 1])
        xs = layer_norm(DEEPNORM_ALPHA * xs + fs, ln_g[i, 1], ln_b[i, 1])
    return (xp, xs, jnp.stack(gm_state_p), jnp.stack(gm_state_s),
            jnp.stack(conv_state_p), jnp.stack(conv_state_s))
```

```python
import functools

import jax
import jax.numpy as jnp
from jax import lax
from jax.experimental import pallas as pl
from jax.experimental.pallas import tpu as pltpu

F32 = jnp.float32
BF16 = jnp.bfloat16

LANES = 128
SUBLANES = 8
LN_EPS = 1e-5
CHUNK = 64
GMLP_CHUNK = 128
TOP_K = 2
VMEM_LIMIT = 56 * 1024 * 1024


def _ln(z, g, b):
    mu = jnp.mean(z, axis=-1, keepdims=True)
    zc = z - mu
    var = jnp.mean(zc * zc, axis=-1, keepdims=True)
    return zc * lax.rsqrt(var + LN_EPS) * g + b


def _gelu(x):
    return 0.5 * x * (1.0 + lax.erf(x * (0.5 ** 0.5)))


def _sigmoid(x):
    return 1.0 / (1.0 + jnp.exp(-x))


def _dot(a, b):
    return jnp.dot(a, b, preferred_element_type=F32)


def _full(shape):
    zeros = (0,) * len(shape)
    return pl.BlockSpec(shape, lambda *_: zeros)


def _params(n_axes=1):
    return pltpu.CompilerParams(
        dimension_semantics=("arbitrary",) * n_axes, vmem_limit_bytes=VMEM_LIMIT)


def _gmlp_body(x_ref, w_in_ref, b_in_ref, lng_ref, lnb_ref, wmix_ref, bmix_ref,
               w_out_ref, b_out_ref, g_ref, b_ref, o_ref, v_ref,
               v_scr, vn_scr, gated_scr, *, tm, dv, groups, alpha, v_rows,
               tiles_per_seq, col_chunk):
    gd = dv // groups
    xb = x_ref[...].astype(BF16)
    n_cc = dv // col_chunk

    s1 = jnp.zeros((tm, 1), F32)
    for c in range(n_cc):
        cols = slice(dv + c * col_chunk, dv + (c + 1) * col_chunk)
        h = _gelu(_dot(xb, w_in_ref[:, cols]) + b_in_ref[:, cols])
        v_scr[:, c * col_chunk:(c + 1) * col_chunk] = h
        s1 = s1 + jnp.sum(h, axis=-1, keepdims=True)
    mu = s1 * (1.0 / dv)
    s2 = jnp.zeros((tm, 1), F32)
    for c in range(n_cc):
        d = v_scr[:, c * col_chunk:(c + 1) * col_chunk] - mu
        s2 = s2 + jnp.sum(d * d, axis=-1, keepdims=True)
    rstd = lax.rsqrt(s2 * (1.0 / dv) + LN_EPS)
    for c in range(n_cc):
        cols = slice(c * col_chunk, (c + 1) * col_chunk)
        vn = (v_scr[:, cols] - mu) * rstd * lng_ref[:, cols] + lnb_ref[:, cols]
        v_scr[:, cols] = vn
        vn_scr[:, cols] = vn.astype(BF16)

    if tiles_per_seq is None:
        v_ref[...] = v_scr[...]
    else:
        @pl.when(pl.program_id(0) % tiles_per_seq == tiles_per_seq - 1)
        def _():
            v_ref[...] = v_scr[tm - v_rows:, :]

    for g in range(groups):
        cols = slice(g * gd, (g + 1) * gd)
        u = _gelu(_dot(xb, w_in_ref[:, cols]) + b_in_ref[:, cols])
        wm = wmix_ref[g]
        bm = bmix_ref[g]
        for c in range(tm // GMLP_CHUNK):
            rows = slice(c * GMLP_CHUNK, (c + 1) * GMLP_CHUNK)
            mixed = _dot(wm, vn_scr[rows, cols]) + bm
            gated_scr[rows, cols] = (u[rows, :] * mixed).astype(BF16)

    y = _dot(gated_scr[...], w_out_ref[...]) + b_out_ref[...]
    o_ref[...] = _ln(alpha * x_ref[...] + y, g_ref[...], b_ref[...])


def _gmlp_layer(x, w_in, b_in, lng, lnb, wmix, bmix, w_out, b_out, g, b, *,
                tm, alpha, seq_len, v_rows):
    n, d = x.shape
    dv = w_out.shape[0]
    groups = wmix.shape[0]
    assert n % tm == 0 and tm % GMLP_CHUNK == 0
    if v_rows is None:
        tiles_per_seq = None
        v_shape = (n, dv)
        v_spec = pl.BlockSpec((tm, dv), lambda i: (i, 0))
    else:
        assert seq_len % tm == 0 and v_rows <= tm
        tiles_per_seq = seq_len // tm
        v_shape = (n // seq_len * v_rows, dv)
        v_spec = pl.BlockSpec((v_rows, dv), lambda i: (i // tiles_per_seq, 0))
    body = functools.partial(
        _gmlp_body, tm=tm, dv=dv, groups=groups, alpha=alpha, v_rows=v_rows,
        tiles_per_seq=tiles_per_seq, col_chunk=768)
    return pl.pallas_call(
        body,
        grid=(n // tm,),
        in_specs=[
            pl.BlockSpec((tm, d), lambda i: (i, 0)),
            _full(w_in.shape), _full(b_in.shape), _full(lng.shape), _full(lnb.shape),
            _full(wmix.shape), _full(bmix.shape), _full(w_out.shape),
            _full(b_out.shape), _full(g.shape), _full(b.shape),
        ],
        out_specs=[pl.BlockSpec((tm, d), lambda i: (i, 0)), v_spec],
        out_shape=[jax.ShapeDtypeStruct((n, d), F32),
                   jax.ShapeDtypeStruct(v_shape, F32)],
        scratch_shapes=[pltpu.VMEM((tm, dv), F32), pltpu.VMEM((tm, dv), BF16),
                        pltpu.VMEM((tm, dv), BF16)],
        compiler_params=_params(),
        name="gmlp_layer",
    )(x, w_in, b_in, lng, lnb, wmix, bmix, w_out, b_out, g, b)


def _ffn_body(x_ref, wg_ref, wu_ref, wd_ref, g_ref, b_ref, o_ref, h_scr, *,
              alpha, f_chunk):
    xb = x_ref[...].astype(BF16)
    ff = wg_ref.shape[1]
    for c in range(ff // f_chunk):
        cols = slice(c * f_chunk, (c + 1) * f_chunk)
        gt = _dot(xb, wg_ref[:, cols])
        up = _dot(xb, wu_ref[:, cols])
        h_scr[:, cols] = (gt * _sigmoid(gt) * up).astype(BF16)
    y = _dot(h_scr[...], wd_ref[...])
    o_ref[...] = _ln(alpha * x_ref[...] + y, g_ref[...], b_ref[...])


def _ffn_layer(x, wg, wu, wd, g, b, *, tm, alpha):
    n, d = x.shape
    ff = wg.shape[1]
    assert n % tm == 0
    body = functools.partial(_ffn_body, alpha=alpha, f_chunk=512)
    return pl.pallas_call(
        body,
        grid=(n // tm,),
        in_specs=[pl.BlockSpec((tm, d), lambda i: (i, 0)), _full(wg.shape),
                  _full(wu.shape), _full(wd.shape), _full(g.shape), _full(b.shape)],
        out_specs=pl.BlockSpec((tm, d), lambda i: (i, 0)),
        out_shape=jax.ShapeDtypeStruct((n, d), F32),
        scratch_shapes=[pltpu.VMEM((tm, ff), BF16)],
        compiler_params=_params(),
        name="ffn_layer",
    )(x, wg, wu, wd, g, b)


HIST_ROWS = 32


def _conv_body(*refs, ts, seq_tiles, use_cache, alpha, width, n_experts, t_chunk):
    if use_cache:
        (x_ref, cache_ref, w1_ref, b1_ref, wdw_ref, bdw_ref, cg_ref, cb_ref, w2_ref,
         b2_ref, g_ref, b_ref, wr_ref, br_ref, cnt_in_ref,
         o_ref, state_ref, ri_ref, rg_ref, cnt_out_ref,
         rows_scr, act_scr, cnt_scr) = refs
    else:
        (x_ref, w1_ref, b1_ref, wdw_ref, bdw_ref, cg_ref, cb_ref, w2_ref,
         b2_ref, g_ref, b_ref, wr_ref, br_ref, cnt_in_ref,
         o_ref, state_ref, ri_ref, rg_ref, cnt_out_ref,
         rows_scr, act_scr, cnt_scr) = refs
        cache_ref = None
    d = x_ref.shape[1]
    nb = d // LANES
    hist = width - 1
    first = HIST_ROWS - hist
    i = pl.program_id(0)

    @pl.when(i == 0)
    def _():
        cnt_scr[...] = cnt_in_ref[...]

    if use_cache:
        for j in range(nb):
            rows_scr[j, 0:HIST_ROWS, :] = cache_ref[0, :, j * LANES:(j + 1) * LANES]
    else:
        @pl.when(i % seq_tiles == 0)
        def _():
            rows_scr[:, 0:HIST_ROWS, :] = jnp.zeros((nb, HIST_ROWS, LANES), F32)

        @pl.when(i % seq_tiles != 0)
        def _():
            rows_scr[:, 0:HIST_ROWS, :] = rows_scr[:, ts:ts + HIST_ROWS, :]

    xb = x_ref[...].astype(BF16)
    cw = 2 * LANES
    for c in range(d // cw):
        a = _dot(xb, w1_ref[:, c * cw:(c + 1) * cw]) + b1_ref[:, c * cw:(c + 1) * cw]
        gt = (_dot(xb, w1_ref[:, d + c * cw:d + (c + 1) * cw])
              + b1_ref[:, d + c * cw:d + (c + 1) * cw])
        glu = a * _sigmoid(gt)
        rows_scr[2 * c, HIST_ROWS:HIST_ROWS + ts, :] = glu[:, :LANES]
        rows_scr[2 * c + 1, HIST_ROWS:HIST_ROWS + ts, :] = glu[:, LANES:]

    for j in range(nb):
        state_ref[0, :, j * LANES:(j + 1) * LANES] = rows_scr[j, ts:ts + HIST_ROWS, :]

    def conv_chunk(tc, carry):
        base = pl.multiple_of(tc * t_chunk, t_chunk)
        ys = []
        s1 = jnp.zeros((t_chunk, 1), F32)
        for j in range(nb):
            acc = jnp.zeros((t_chunk, LANES), F32)
            for k in range(width):
                tap = wdw_ref[j, k:k + 1, :]
                acc = acc + rows_scr[j, pl.ds(base + first + k, t_chunk), :] * tap
            acc = acc + bdw_ref[:, j * LANES:(j + 1) * LANES]
            ys.append(acc)
            s1 = s1 + jnp.sum(acc, axis=-1, keepdims=True)
        mu = s1 * (1.0 / d)
        s2 = jnp.zeros((t_chunk, 1), F32)
        for j in range(nb):
            dj = ys[j] - mu
            s2 = s2 + jnp.sum(dj * dj, axis=-1, keepdims=True)
        rstd = lax.rsqrt(s2 * (1.0 / d) + LN_EPS)
        for j in range(nb):
            cols = slice(j * LANES, (j + 1) * LANES)
            yn = (ys[j] - mu) * rstd * cg_ref[:, cols] + cb_ref[:, cols]
            act_scr[pl.ds(base, t_chunk), cols] = (yn * _sigmoid(yn)).astype(BF16)
        return carry

    lax.fori_loop(0, ts // t_chunk, conv_chunk, 0)

    m = _dot(act_scr[...], w2_ref[...]) + b2_ref[...]
    x3 = _ln(alpha * x_ref[...] + m, g_ref[...], b_ref[...])
    o_ref[...] = x3

    logits = jnp.dot(x3, wr_ref[...], preferred_element_type=F32,
                     precision=lax.Precision.HIGHEST) + br_ref[...]
    lane = lax.broadcasted_iota(jnp.int32, (ts, LANES), 1)
    neg = jnp.float32(-jnp.inf)
    logits = jnp.where(lane < n_experts, logits, neg)
    m1 = jnp.max(logits, axis=-1, keepdims=True)
    i1 = jnp.min(jnp.where(logits == m1, lane, LANES), axis=-1, keepdims=True)
    rest = jnp.where(lane == i1, neg, logits)
    m2 = jnp.max(rest, axis=-1, keepdims=True)
    i2 = jnp.min(jnp.where(rest == m2, lane, LANES), axis=-1, keepdims=True)
    e2 = jnp.exp(m2 - m1)
    den = 1.0 + e2
    g1 = 1.0 / den
    g2 = e2 / den
    sel1 = lane == i1
    sel2 = lane == i2
    onehot = jnp.where(sel1 | sel2, 1.0, 0.0)
    r_io = lax.broadcasted_iota(jnp.int32, (ts, ts), 0)
    c_io = lax.broadcasted_iota(jnp.int32, (ts, ts), 1)
    tri = jnp.where(c_io < r_io, 1.0, 0.0).astype(BF16)
    prefix = _dot(tri, onehot.astype(BF16)) + cnt_scr[...]
    p1 = jnp.sum(jnp.where(sel1, prefix, 0.0), axis=-1, keepdims=True).astype(jnp.int32)
    p2 = jnp.sum(jnp.where(sel2, prefix, 0.0), axis=-1, keepdims=True).astype(jnp.int32)
    cnt_new = cnt_scr[...] + jnp.sum(onehot, axis=0, keepdims=True)
    cnt_scr[...] = cnt_new
    cnt_out_ref[...] = cnt_new
    ri_ref[...] = jnp.where(lane == 0, i1, jnp.where(lane == 1, i2,
                  jnp.where(lane == 2, p1, jnp.where(lane == 3, p2, 0))))
    rg_ref[...] = jnp.where(lane == 0, g1, jnp.where(lane == 1, g2, 0.0))


def _conv_layer(x, cache, w1, b1, wdw, bdw, cg, cb, w2, b2, g, b, wr, br, cnt_in, *,
                ts, seq_len, alpha, width, n_experts):
    n, d = x.shape
    nb = d // LANES
    hist = HIST_ROWS
    assert seq_len % ts == 0 and width - 1 <= HIST_ROWS <= ts
    seq_tiles = seq_len // ts
    use_cache = cache is not None
    assert not use_cache or seq_tiles == 1
    t_chunk = min(ts, 128)
    body = functools.partial(
        _conv_body, ts=ts, seq_tiles=seq_tiles, use_cache=use_cache, alpha=alpha,
        width=width, n_experts=n_experts, t_chunk=t_chunk)
    weights = (w1, b1, wdw, bdw, cg, cb, w2, b2, g, b, wr, br, cnt_in)
    in_specs = [pl.BlockSpec((ts, d), lambda i: (i, 0))]
    args = [x]
    if use_cache:
        in_specs.append(pl.BlockSpec((1, hist, d), lambda i: (i, 0, 0)))
        args.append(cache)
    in_specs += [_full(w.shape) for w in weights]
    args += list(weights)
    n_seq = n // seq_len
    return pl.pallas_call(
        body,
        grid=(n // ts,),
        in_specs=in_specs,
        out_specs=[
            pl.BlockSpec((ts, d), lambda i: (i, 0)),
            pl.BlockSpec((1, hist, d), lambda i: (i // seq_tiles, 0, 0)),
            pl.BlockSpec((ts, LANES), lambda i: (i, 0)),
            pl.BlockSpec((ts, LANES), lambda i: (i, 0)),
            _full((1, LANES)),
        ],
        out_shape=[
            jax.ShapeDtypeStruct((n, d), F32),
            jax.ShapeDtypeStruct((n_seq, hist, d), F32),
            jax.ShapeDtypeStruct((n, LANES), jnp.int32),
            jax.ShapeDtypeStruct((n, LANES), F32),
            jax.ShapeDtypeStruct((1, LANES), F32),
        ],
        scratch_shapes=[pltpu.VMEM((nb, ts + HIST_ROWS, LANES), F32),
                        pltpu.VMEM((ts, d), BF16), pltpu.VMEM((1, LANES), F32)],
        compiler_params=_params(),
        name="conv_layer",
    )(*args)


def _dispatch_body(dest_a_ref, dest_b_ref, fill_ref, nu_ref, xa_hbm, xb_hbm, zeros_hbm,
                   out_hbm, sem, *, tb, steps_a, n_fill, n_tiles, tile_rows):
    i = pl.program_id(0)

    def fill(start):
        start = pl.multiple_of(start, tile_rows)
        return pltpu.make_async_copy(zeros_hbm, out_hbm.at[pl.ds(start, tile_rows)], sem)

    @pl.when(i == 0)
    def _():
        for e in range(n_fill):
            fill(fill_ref[e]).start()
        for e in range(n_fill):
            fill(fill_ref[e]).wait()

        def fill_unused(j, carry):
            fill(j * tile_rows).start()
            fill(j * tile_rows).wait()
            return carry

        lax.fori_loop(nu_ref[0], n_tiles, fill_unused, 0)

    def scatter(x_hbm, dest_ref, base):
        def copy(r, k):
            return pltpu.make_async_copy(
                x_hbm.at[pl.ds(base + r, 1)],
                out_hbm.at[pl.ds(dest_ref[TOP_K * r + k], 1)], sem)

        def issue(r, carry):
            for k in range(TOP_K):
                copy(r, k).start()
            return carry

        def drain(r, carry):
            for k in range(TOP_K):
                copy(r, k).wait()
            return carry

        lax.fori_loop(0, tb, issue, 0, unroll=8)
        lax.fori_loop(0, tb, drain, 0, unroll=8)

    @pl.when(i < steps_a)
    def _():
        scatter(xa_hbm, dest_a_ref, i * tb)

    @pl.when(i >= steps_a)
    def _():
        scatter(xb_hbm, dest_b_ref, (i - steps_a) * tb)


def _dispatch(xa, dest_a, xb, dest_b, zeros, fill_starts, n_used, *, tb, n_tiles):
    (na, d), nb = xa.shape, xb.shape[0]
    tile_rows = zeros.shape[0]
    assert na % tb == 0 and nb % tb == 0
    steps_a, steps_b = na // tb, nb // tb
    smem = pl.BlockSpec(memory_space=pltpu.SMEM)
    any_spec = pl.BlockSpec(memory_space=pl.ANY)
    body = functools.partial(
        _dispatch_body, tb=tb, steps_a=steps_a, n_fill=fill_starts.shape[0],
        n_tiles=n_tiles, tile_rows=tile_rows)
    return pl.pallas_call(
        body,
        grid=(steps_a + steps_b,),
        in_specs=[
            pl.BlockSpec((TOP_K * tb,), lambda i: (jnp.minimum(i, steps_a - 1),),
                         memory_space=pltpu.SMEM),
            pl.BlockSpec((TOP_K * tb,), lambda i: (jnp.maximum(i - steps_a, 0),),
                         memory_space=pltpu.SMEM),
            smem, smem, any_spec, any_spec, any_spec,
        ],
        out_specs=any_spec,
        out_shape=jax.ShapeDtypeStruct((n_tiles * tile_rows, d), xa.dtype),
        scratch_shapes=[pltpu.SemaphoreType.DMA(())],
        compiler_params=_params(),
        name="moe_dispatch",
    )(dest_a, dest_b, fill_starts, n_used, xa, xb, zeros)


def _moe_body(tile_ref, te_ref, nu_ref, xs_ref, wg_ref, wu_ref, wd_ref, o_ref, xb_scr):
    del tile_ref, te_ref
    i = pl.program_id(0)
    f = pl.program_id(1)

    @pl.when(i < nu_ref[0])
    def _():
        @pl.when(f == 0)
        def _():
            xb_scr[...] = xs_ref[...].astype(BF16)

        xb = xb_scr[...]
        gt = _dot(xb, wg_ref[0])
        up = _dot(xb, wu_ref[0])
        h = (gt * _sigmoid(gt) * up).astype(BF16)
        y = _dot(h, wd_ref[0])

        @pl.when(f == 0)
        def _():
            o_ref[...] = y

        @pl.when(f != 0)
        def _():
            o_ref[...] += y

    @pl.when((i >= nu_ref[0]) & (f == 0))
    def _():
        o_ref[...] = jnp.zeros(o_ref.shape, o_ref.dtype)


def _moe_grouped(xs, tile_idx, tile_expert, n_used, wg, wu, wd, *, tm, f_chunk):
    r, d = xs.shape
    ff = wg.shape[2]
    assert r % tm == 0 and ff % f_chunk == 0
    grid_spec = pltpu.PrefetchScalarGridSpec(
        num_scalar_prefetch=3,
        grid=(r // tm, ff // f_chunk),
        in_specs=[
            pl.BlockSpec((tm, d), lambda i, f, ti, te, nu: (ti[i], 0)),
            pl.BlockSpec((1, d, f_chunk), lambda i, f, ti, te, nu: (te[i], 0, f)),
            pl.BlockSpec((1, d, f_chunk), lambda i, f, ti, te, nu: (te[i], 0, f)),
            pl.BlockSpec((1, f_chunk, d), lambda i, f, ti, te, nu: (te[i], f, 0)),
        ],
        out_specs=pl.BlockSpec((tm, d), lambda i, f, ti, te, nu: (i, 0)),
        scratch_shapes=[pltpu.VMEM((tm, d), BF16)],
    )
    return pl.pallas_call(
        _moe_body,
        grid_spec=grid_spec,
        out_shape=jax.ShapeDtypeStruct((r, d), F32),
        compiler_params=_params(2),
        name="moe_grouped",
    )(tile_idx, tile_expert, n_used, xs, wg, wu, wd)


def _combine_body(dest_ref, x_ref, rg_ref, ys_hbm, g_ref, b_ref, o_ref, y_scr, sem, *,
                  tb, alpha):
    def issue(r, carry):
        for k in range(TOP_K):
            pltpu.make_async_copy(
                ys_hbm.at[pl.ds(dest_ref[TOP_K * r + k], 1)],
                y_scr.at[k, pl.ds(r, 1)], sem).start()
        return carry

    lax.fori_loop(0, tb, issue, 0, unroll=8)

    def drain(r, carry):
        for k in range(TOP_K):
            pltpu.make_async_copy(
                ys_hbm.at[pl.ds(dest_ref[TOP_K * r + k], 1)],
                y_scr.at[k, pl.ds(r, 1)], sem).wait()
        return carry

    lax.fori_loop(0, tb, drain, 0, unroll=8)

    rg = rg_ref[...]
    moe = rg[:, 0:1] * y_scr[0] + rg[:, 1:2] * y_scr[1]
    o_ref[...] = _ln(alpha * x_ref[...] + moe, g_ref[...], b_ref[...])


def _combine(x, rg, dest, ys, g, b, *, tb, alpha):
    n, d = x.shape
    assert n % tb == 0
    body = functools.partial(_combine_body, tb=tb, alpha=alpha)
    return pl.pallas_call(
        body,
        grid=(n // tb,),
        in_specs=[
            pl.BlockSpec((TOP_K * tb,), lambda i: (i,), memory_space=pltpu.SMEM),
            pl.BlockSpec((tb, d), lambda i: (i, 0)),
            pl.BlockSpec((tb, LANES), lambda i: (i, 0)),
            pl.BlockSpec(memory_space=pl.ANY),
            _full(g.shape), _full(b.shape),
        ],
        out_specs=pl.BlockSpec((tb, d), lambda i: (i, 0)),
        out_shape=jax.ShapeDtypeStruct((n, d), F32),
        scratch_shapes=[pltpu.VMEM((TOP_K, tb, d), F32), pltpu.SemaphoreType.DMA(())],
        compiler_params=_params(),
        name="moe_combine",
    )(dest, x, rg, ys, g, b)


MOE_TILE = 1024
MOE_F_CHUNK = 512


def _row(v):
    return v.reshape(1, -1).astype(F32)


def kernel(x_prompt, x_sample, cache_conv, gm_w_in, gm_b_in, gm_lnv_g, gm_lnv_b, gm_w_s, gm_b_s, gm_w_out, gm_b_out, cv_w_pw1, cv_b_pw1, cv_w_dw, cv_b_dw, cv_ln_g, cv_ln_b, cv_w_pw2, cv_b_pw2, ff_w_gate, ff_w_up, ff_w_down, moe_w_router, moe_b_router, moe_w_gate, moe_w_up, moe_w_down, ln_g, ln_b):
    bp, sp, d = x_prompt.shape
    bs, ss, _ = x_sample.shape
    depth = ln_g.shape[0]
    assert depth == 2, "one gMLP layer followed by one conv/MoE layer"
    alpha = (2.0 * depth) ** 0.25
    n_experts = moe_w_router.shape[-1]
    width = cv_w_dw.shape[1]
    groups = gm_w_s.shape[1]
    np_, ns_ = bp * sp, bs * ss
    xp = x_prompt.reshape(np_, d)
    xs_ = x_sample.reshape(ns_, d)

    w_s = gm_w_s[0]
    pos = jnp.arange(GMLP_CHUNK) // CHUNK
    mask = pos[None, :] <= pos[:, None]
    wmix_p = jnp.where(mask[None], w_s, 0.0).astype(BF16)
    bmix_p = gm_b_s[0][:, :, None].astype(F32)
    assert GMLP_CHUNK % ss == 0 and ss <= CHUNK
    rep = GMLP_CHUNK // ss
    blk = w_s[:, :ss, :ss]
    eye = jnp.eye(rep, dtype=F32)
    wmix_s = jnp.einsum("ab,gij->gaibj", eye, blk).reshape(
        groups, GMLP_CHUNK, GMLP_CHUNK).astype(BF16)
    bmix_s = jnp.tile(gm_b_s[0][:, :ss], (1, rep))[:, :, None].astype(F32)

    gm_common = dict(alpha=alpha)
    gm_w = (gm_w_in[0].astype(BF16), _row(gm_b_in[0]), _row(gm_lnv_g[0]),
            _row(gm_lnv_b[0]))
    gm_tail = (gm_w_out[0].astype(BF16), _row(gm_b_out[0]), _row(ln_g[0, 0]),
               _row(ln_b[0, 0]))
    xp, vp = _gmlp_layer(xp, *gm_w, wmix_p, bmix_p, *gm_tail, tm=512, seq_len=sp,
                         v_rows=GMLP_CHUNK, **gm_common)
    xs_, vs = _gmlp_layer(xs_, *gm_w, wmix_s, bmix_s, *gm_tail, tm=256, seq_len=ss,
                          v_rows=None, **gm_common)
    gm_state_p = vp.reshape(1, bp, GMLP_CHUNK, -1)
    gm_state_s = vs.reshape(1, bs, ss, -1)

    ff_w = (ff_w_gate[0].astype(BF16), ff_w_up[0].astype(BF16),
            ff_w_down[0].astype(BF16), _row(ln_g[0, 1]), _row(ln_b[0, 1]))
    xp = _ffn_layer(xp, *ff_w, tm=512, alpha=alpha)
    xs_ = _ffn_layer(xs_, *ff_w, tm=512, alpha=alpha)

    nb = d // LANES
    wdw = jnp.transpose(cv_w_dw[0].reshape(width, nb, LANES), (1, 0, 2)).astype(F32)
    wr = jnp.zeros((d, LANES), F32).at[:, :n_experts].set(moe_w_router[0])
    br = jnp.zeros((1, LANES), F32).at[0, :n_experts].set(moe_b_router[0])
    cv_w = (cv_w_pw1[0].astype(BF16), _row(cv_b_pw1[0]), wdw, _row(cv_b_dw[0]),
            _row(cv_ln_g[0]), _row(cv_ln_b[0]), cv_w_pw2[0].astype(BF16),
            _row(cv_b_pw2[0]), _row(ln_g[1, 0]), _row(ln_b[1, 0]), wr, br)
    cv_common = dict(alpha=alpha, width=width, n_experts=n_experts)
    cnt0 = jnp.zeros((1, LANES), F32)
    xp, st_p, ri_p, rg_p, cnt_p = _conv_layer(
        xp, None, *cv_w, cnt0, ts=512, seq_len=sp, **cv_common)
    hist = width - 1
    cache = jnp.pad(cache_conv[0].astype(F32), ((0, 0), (HIST_ROWS - hist, 0), (0, 0)))
    xs_, st_s, ri_s, rg_s, cnt_s = _conv_layer(
        xs_, cache, *cv_w, cnt_p, ts=ss, seq_len=ss, **cv_common)
    conv_state_p = st_p[None, :, HIST_ROWS - hist:, :]
    conv_state_s = st_s[None, :, HIST_ROWS - hist:, :]

    tm = MOE_TILE
    n_assign = (np_ + ns_) * TOP_K
    n_tiles = (n_assign + n_experts * (tm - 1)) // tm
    counts = cnt_s[0, :n_experts].astype(jnp.int32)
    tiles_e = (counts + tm - 1) // tm
    tile_end = jnp.cumsum(tiles_e)
    offs = (tile_end - tiles_e) * tm
    n_used = tile_end[-1:]
    tile_idx = jnp.minimum(jnp.arange(n_tiles, dtype=jnp.int32), n_used[0] - 1)
    tile_expert = jnp.minimum(
        jnp.sum(tile_idx[:, None] >= tile_end[None, :], axis=1), n_experts - 1
    ).astype(jnp.int32)
    fill_starts = (jnp.maximum(tile_end, 1) - 1).astype(jnp.int32) * tm

    def dest_of(ri):
        e_idx = ri[:, 0:TOP_K]
        sel = e_idx[:, :, None] == jnp.arange(n_experts, dtype=jnp.int32)
        start = jnp.sum(jnp.where(sel, offs.astype(jnp.int32), 0), axis=-1)
        return (start + ri[:, TOP_K:2 * TOP_K]).reshape(-1).astype(jnp.int32)

    dest_p, dest_s = dest_of(ri_p), dest_of(ri_s)
    zeros = jnp.zeros((tm, d), F32)
    n_used = n_used.astype(jnp.int32)
    xsort = _dispatch(xp, dest_p, xs_, dest_s, zeros, fill_starts, n_used, tb=1024,
                      n_tiles=n_tiles)
    ysort = _moe_grouped(
        xsort, tile_idx, tile_expert, n_used,
        moe_w_gate[0].astype(BF16), moe_w_up[0].astype(BF16),
        moe_w_down[0].astype(BF16), tm=tm, f_chunk=MOE_F_CHUNK)
    fin = (_row(ln_g[1, 1]), _row(ln_b[1, 1]))
    yp = _combine(xp, rg_p, dest_p, ysort, *fin, tb=512, alpha=alpha)
    ys_ = _combine(xs_, rg_s, dest_s, ysort, *fin, tb=512, alpha=alpha)

    return (yp.reshape(bp, sp, d), ys_.reshape(bs, ss, d), gm_state_p, gm_state_s,
            conv_state_p, conv_state_s)
```

```python
import functools

import jax
import jax.numpy as jnp
from jax import lax
from jax.experimental import pallas as pl
from jax.experimental.pallas import tpu as pltpu

F32 = jnp.float32
BF16 = jnp.bfloat16

LANES = 128
SUBLANES = 8
LN_EPS = 1e-5
CHUNK = 64
GMLP_CHUNK = 128
TOP_K = 2
VMEM_LIMIT = 56 * 1024 * 1024


def _ln(z, g, b):
    mu = jnp.mean(z, axis=-1, keepdims=True)
    zc = z - mu
    var = jnp.mean(zc * zc, axis=-1, keepdims=True)
    return zc * lax.rsqrt(var + LN_EPS) * g + b


def _gelu(x):
    return 0.5 * x * (1.0 + lax.erf(x * (0.5 ** 0.5)))


def _sigmoid(x):
    return 1.0 / (1.0 + jnp.exp(-x))


def _dot(a, b):
    return jnp.dot(a, b, preferred_element_type=F32)


def _load_row_tiles(ref, rows):
    return jnp.concatenate(
        [ref[pl.ds(j, rows, stride=SUBLANES), :] for j in range(SUBLANES)], axis=1)


def _row_tile(ref, r):
    return ref.at[pl.ds(pl.multiple_of(r * SUBLANES, SUBLANES), SUBLANES)]


def _store_row_tiles(ref, value):
    rows = value.shape[0]
    for j in range(SUBLANES):
        ref[pl.ds(j, rows, stride=SUBLANES), :] = value[:, j * LANES:(j + 1) * LANES]


def _full(shape):
    zeros = (0,) * len(shape)
    return pl.BlockSpec(shape, lambda *_: zeros)


def _params(n_axes=1):
    return pltpu.CompilerParams(
        dimension_semantics=("arbitrary",) * n_axes, vmem_limit_bytes=VMEM_LIMIT)


def _gmlp_body(x_ref, w_in_ref, b_in_ref, lng_ref, lnb_ref, wmix_ref, bmix_ref,
               w_out_ref, b_out_ref, g_ref, b_ref, o_ref, v_ref,
               v_scr, vn_scr, gated_scr, *, tm, dv, groups, alpha, v_rows,
               col_chunk):
    gd = dv // groups
    xb = x_ref[...].astype(BF16)
    n_cc = dv // col_chunk

    s1 = jnp.zeros((tm, 1), F32)
    for c in range(n_cc):
        cols = slice(dv + c * col_chunk, dv + (c + 1) * col_chunk)
        h = _gelu(_dot(xb, w_in_ref[:, cols]) + b_in_ref[:, cols])
        v_scr[:, c * col_chunk:(c + 1) * col_chunk] = h
        s1 = s1 + jnp.sum(h, axis=-1, keepdims=True)
    mu = s1 * (1.0 / dv)
    s2 = jnp.zeros((tm, 1), F32)
    for c in range(n_cc):
        d = v_scr[:, c * col_chunk:(c + 1) * col_chunk] - mu
        s2 = s2 + jnp.sum(d * d, axis=-1, keepdims=True)
    rstd = lax.rsqrt(s2 * (1.0 / dv) + LN_EPS)
    keep = tm if v_rows is None else v_rows
    for c in range(n_cc):
        cols = slice(c * col_chunk, (c + 1) * col_chunk)
        vn = (v_scr[:, cols] - mu) * rstd * lng_ref[:, cols] + lnb_ref[:, cols]
        v_ref[:, cols] = vn[tm - keep:, :]
        vn_scr[:, cols] = vn.astype(BF16)

    n_ch = tm // GMLP_CHUNK
    for g in range(groups):
        cols = slice(g * gd, (g + 1) * gd)
        u = _gelu(_dot(xb, w_in_ref[:, cols]) + b_in_ref[:, cols])
        vg = jnp.concatenate(
            [vn_scr[c * GMLP_CHUNK:(c + 1) * GMLP_CHUNK, cols] for c in range(n_ch)], axis=1)
        mixed = _dot(wmix_ref[g], vg) + bmix_ref[g]
        for c in range(n_ch):
            rows = slice(c * GMLP_CHUNK, (c + 1) * GMLP_CHUNK)
            gated_scr[rows, cols] = (u[rows, :] * mixed[:, c * gd:(c + 1) * gd]).astype(BF16)

    y = _dot(gated_scr[...], w_out_ref[...]) + b_out_ref[...]
    o_ref[...] = _ln(alpha * x_ref[...] + y, g_ref[...], b_ref[...])


def _gmlp_layer(x, w_in, b_in, lng, lnb, wmix, bmix, w_out, b_out, g, b, *,
                tm, alpha, seq_len, v_rows, col_chunk=768):
    n, d = x.shape
    dv = w_out.shape[0]
    groups = wmix.shape[0]
    assert n % tm == 0 and tm % GMLP_CHUNK == 0
    if v_rows is None:
        tiles_per_seq = None
        v_shape = (n, dv)
        v_spec = pl.BlockSpec((tm, dv), lambda i: (i, 0))
    else:
        assert seq_len % tm == 0 and v_rows <= tm
        tiles_per_seq = seq_len // tm
        v_shape = (n // seq_len * v_rows, dv)
        v_spec = pl.BlockSpec((v_rows, dv), lambda i: (i // tiles_per_seq, 0))
    body = functools.partial(
        _gmlp_body, tm=tm, dv=dv, groups=groups, alpha=alpha, v_rows=v_rows,
        col_chunk=col_chunk)
    return pl.pallas_call(
        body,
        grid=(n // tm,),
        in_specs=[
            pl.BlockSpec((tm, d), lambda i: (i, 0)),
            _full(w_in.shape), _full(b_in.shape), _full(lng.shape), _full(lnb.shape),
            _full(wmix.shape), _full(bmix.shape), _full(w_out.shape),
            _full(b_out.shape), _full(g.shape), _full(b.shape),
        ],
        out_specs=[pl.BlockSpec((tm, d), lambda i: (i, 0)), v_spec],
        out_shape=[jax.ShapeDtypeStruct((n, d), F32),
                   jax.ShapeDtypeStruct(v_shape, F32)],
        scratch_shapes=[pltpu.VMEM((tm, dv), F32), pltpu.VMEM((tm, dv), BF16),
                        pltpu.VMEM((tm, dv), BF16)],
        compiler_params=_params(),
        name="gmlp_layer",
    )(x, w_in, b_in, lng, lnb, wmix, bmix, w_out, b_out, g, b)


def _ffn_body(x_ref, wg_ref, wu_ref, wd_ref, g_ref, b_ref, o_ref, h_scr, *,
              alpha, f_chunk):
    xb = x_ref[...].astype(BF16)
    ff = wg_ref.shape[1]
    for c in range(ff // f_chunk):
        cols = slice(c * f_chunk, (c + 1) * f_chunk)
        gt = _dot(xb, wg_ref[:, cols])
        up = _dot(xb, wu_ref[:, cols])
        h_scr[:, cols] = (gt * _sigmoid(gt) * up).astype(BF16)
    y = _dot(h_scr[...], wd_ref[...])
    o_ref[...] = _ln(alpha * x_ref[...] + y, g_ref[...], b_ref[...])


def _ffn_layer(x, wg, wu, wd, g, b, *, tm, alpha):
    n, d = x.shape
    ff = wg.shape[1]
    assert n % tm == 0
    body = functools.partial(_ffn_body, alpha=alpha, f_chunk=512)
    return pl.pallas_call(
        body,
        grid=(n // tm,),
        in_specs=[pl.BlockSpec((tm, d), lambda i: (i, 0)), _full(wg.shape),
                  _full(wu.shape), _full(wd.shape), _full(g.shape), _full(b.shape)],
        out_specs=pl.BlockSpec((tm, d), lambda i: (i, 0)),
        out_shape=jax.ShapeDtypeStruct((n, d), F32),
        scratch_shapes=[pltpu.VMEM((tm, ff), BF16)],
        compiler_params=_params(),
        name="ffn_layer",
    )(x, wg, wu, wd, g, b)


HIST_ROWS = 32


def _conv_body(*refs, ts, seq_tiles, use_cache, alpha, width, n_experts, t_chunk):
    if use_cache:
        (x_ref, cache_ref, w1_ref, b1_ref, wdw_ref, bdw_ref, cg_ref, cb_ref, w2_ref,
         b2_ref, g_ref, b_ref, wr_ref, br_ref, cnt_in_ref,
         o_ref, state_ref, ri_ref, rg_ref, cnt_out_ref,
         rows_scr, act_scr, cnt_scr) = refs
    else:
        (x_ref, w1_ref, b1_ref, wdw_ref, bdw_ref, cg_ref, cb_ref, w2_ref,
         b2_ref, g_ref, b_ref, wr_ref, br_ref, cnt_in_ref,
         o_ref, state_ref, ri_ref, rg_ref, cnt_out_ref,
         rows_scr, act_scr, cnt_scr) = refs
        cache_ref = None
    d = x_ref.shape[1]
    nb = d // LANES
    hist = width - 1
    first = HIST_ROWS - hist
    i = pl.program_id(0)

    @pl.when(i == 0)
    def _():
        cnt_scr[...] = cnt_in_ref[...]

    if use_cache:
        for j in range(nb):
            rows_scr[j, 0:HIST_ROWS, :] = cache_ref[0, :, j * LANES:(j + 1) * LANES]
    else:
        @pl.when(i % seq_tiles == 0)
        def _():
            rows_scr[:, 0:HIST_ROWS, :] = jnp.zeros((nb, HIST_ROWS, LANES), F32)

        @pl.when(i % seq_tiles != 0)
        def _():
            rows_scr[:, 0:HIST_ROWS, :] = rows_scr[:, ts:ts + HIST_ROWS, :]

    xb = x_ref[...].astype(BF16)
    cw = 2 * LANES
    for c in range(d // cw):
        a = _dot(xb, w1_ref[:, c * cw:(c + 1) * cw]) + b1_ref[:, c * cw:(c + 1) * cw]
        gt = (_dot(xb, w1_ref[:, d + c * cw:d + (c + 1) * cw])
              + b1_ref[:, d + c * cw:d + (c + 1) * cw])
        glu = a * _sigmoid(gt)
        rows_scr[2 * c, HIST_ROWS:HIST_ROWS + ts, :] = glu[:, :LANES]
        rows_scr[2 * c + 1, HIST_ROWS:HIST_ROWS + ts, :] = glu[:, LANES:]

    for j in range(nb):
        state_ref[0, :, j * LANES:(j + 1) * LANES] = rows_scr[j, ts:ts + HIST_ROWS, :]

    def conv_chunk(tc, carry):
        base = tc * t_chunk if isinstance(tc, int) else pl.multiple_of(tc * t_chunk, t_chunk)
        ys = []
        s1 = jnp.zeros((t_chunk, 1), F32)
        for j in range(nb):
            acc = jnp.zeros((t_chunk, LANES), F32)
            for k in range(width):
                tap = wdw_ref[j, k:k + 1, :]
                acc = acc + rows_scr[j, pl.ds(base + first + k, t_chunk), :] * tap
            acc = acc + bdw_ref[:, j * LANES:(j + 1) * LANES]
            ys.append(acc)
            s1 = s1 + jnp.sum(acc, axis=-1, keepdims=True)
        mu = s1 * (1.0 / d)
        s2 = jnp.zeros((t_chunk, 1), F32)
        for j in range(nb):
            dj = ys[j] - mu
            s2 = s2 + jnp.sum(dj * dj, axis=-1, keepdims=True)
        rstd = lax.rsqrt(s2 * (1.0 / d) + LN_EPS)
        for j in range(nb):
            cols = slice(j * LANES, (j + 1) * LANES)
            yn = (ys[j] - mu) * rstd * cg_ref[:, cols] + cb_ref[:, cols]
            act_scr[pl.ds(base, t_chunk), cols] = (yn * _sigmoid(yn)).astype(BF16)
        return carry

    for tc in range(ts // t_chunk):
        conv_chunk(tc, 0)

    m = _dot(act_scr[...], w2_ref[...]) + b2_ref[...]
    x3 = _ln(alpha * x_ref[...] + m, g_ref[...], b_ref[...])
    _store_row_tiles(o_ref, x3)

    x3_hi = x3.astype(BF16)
    x3_lo = (x3 - x3_hi.astype(F32)).astype(BF16)
    hi_prod = _dot(x3_hi, wr_ref[...])
    logits = (hi_prod[:, :LANES] + (hi_prod[:, LANES:] + _dot(x3_lo, wr_ref[:, :LANES]))
              + br_ref[...])
    lane = lax.broadcasted_iota(jnp.int32, (ts, LANES), 1)
    neg = jnp.float32(-jnp.inf)
    logits = jnp.where(lane < n_experts, logits, neg)
    m1 = jnp.max(logits, axis=-1, keepdims=True)
    i1 = jnp.min(jnp.where(logits == m1, lane, LANES), axis=-1, keepdims=True)
    rest = jnp.where(lane == i1, neg, logits)
    m2 = jnp.max(rest, axis=-1, keepdims=True)
    i2 = jnp.min(jnp.where(rest == m2, lane, LANES), axis=-1, keepdims=True)
    e2 = jnp.exp(m2 - m1)
    den = 1.0 + e2
    g1 = 1.0 / den
    g2 = e2 / den
    sel1 = lane == i1
    sel2 = lane == i2
    onehot = jnp.where(sel1 | sel2, 1.0, 0.0)
    r_io = lax.broadcasted_iota(jnp.int32, (ts, ts), 0)
    c_io = lax.broadcasted_iota(jnp.int32, (ts, ts), 1)
    tri = jnp.where(c_io < r_io, 1.0, 0.0).astype(BF16)
    prefix = _dot(tri, onehot.astype(BF16)) + cnt_scr[...]
    p1 = jnp.sum(jnp.where(sel1, prefix, 0.0), axis=-1, keepdims=True).astype(jnp.int32)
    p2 = jnp.sum(jnp.where(sel2, prefix, 0.0), axis=-1, keepdims=True).astype(jnp.int32)
    cnt_new = cnt_scr[...] + jnp.sum(onehot, axis=0, keepdims=True)
    cnt_scr[...] = cnt_new
    cnt_out_ref[...] = cnt_new
    ri_ref[...] = jnp.where(lane == 0, i1, jnp.where(lane == 1, i2,
                  jnp.where(lane == 2, p1, jnp.where(lane == 3, p2, 0))))
    rg_ref[...] = jnp.where(lane == 0, g1, jnp.where(lane == 1, g2, 0.0))


def _conv_layer(x, cache, w1, b1, wdw, bdw, cg, cb, w2, b2, g, b, wr, br, cnt_in, *,
                ts, seq_len, alpha, width, n_experts):
    n, d = x.shape
    nb = d // LANES
    hist = HIST_ROWS
    assert seq_len % ts == 0 and width - 1 <= HIST_ROWS <= ts and nb == SUBLANES
    seq_tiles = seq_len // ts
    use_cache = cache is not None
    assert not use_cache or seq_tiles == 1
    t_chunk = min(ts, 128)
    body = functools.partial(
        _conv_body, ts=ts, seq_tiles=seq_tiles, use_cache=use_cache, alpha=alpha,
        width=width, n_experts=n_experts, t_chunk=t_chunk)
    weights = (w1, b1, wdw, bdw, cg, cb, w2, b2, g, b, wr, br, cnt_in)
    in_specs = [pl.BlockSpec((ts, d), lambda i: (i, 0))]
    args = [x]
    if use_cache:
        in_specs.append(pl.BlockSpec((1, hist, d), lambda i: (i, 0, 0)))
        args.append(cache)
    in_specs += [_full(w.shape) for w in weights]
    args += list(weights)
    n_seq = n // seq_len
    return pl.pallas_call(
        body,
        grid=(n // ts,),
        in_specs=in_specs,
        out_specs=[
            pl.BlockSpec((ts * nb, LANES), lambda i: (i, 0)),
            pl.BlockSpec((1, hist, d), lambda i: (i // seq_tiles, 0, 0)),
            pl.BlockSpec((ts, LANES), lambda i: (i, 0)),
            pl.BlockSpec((ts, LANES), lambda i: (i, 0)),
            _full((1, LANES)),
        ],
        out_shape=[
            jax.ShapeDtypeStruct((n * nb, LANES), F32),
            jax.ShapeDtypeStruct((n_seq, hist, d), F32),
            jax.ShapeDtypeStruct((n, LANES), jnp.int32),
            jax.ShapeDtypeStruct((n, LANES), F32),
            jax.ShapeDtypeStruct((1, LANES), F32),
        ],
        scratch_shapes=[pltpu.VMEM((nb, ts + HIST_ROWS, LANES), F32),
                        pltpu.VMEM((ts, d), BF16), pltpu.VMEM((1, LANES), F32)],
        compiler_params=_params(),
        name="conv_layer",
    )(*args)


def _dispatch_body(dest_a_ref, dest_b_ref, fill_ref, nu_ref, xa_ref, xb_ref, zeros_hbm,
                   out_hbm, sem, *, tb, steps_a, n_fill, n_tiles, tile_rows):
    i = pl.program_id(0)

    def fill(start):
        start = pl.multiple_of(start * SUBLANES, tile_rows * SUBLANES)
        return pltpu.make_async_copy(
            zeros_hbm, out_hbm.at[pl.ds(start, tile_rows * SUBLANES)], sem)

    @pl.when(i == 0)
    def _():
        for e in range(n_fill):
            fill(fill_ref[e]).start()
        for e in range(n_fill):
            fill(fill_ref[e]).wait()

        def fill_unused(j, carry):
            fill(j * tile_rows).start()
            fill(j * tile_rows).wait()
            return carry

        lax.fori_loop(nu_ref[0], n_tiles, fill_unused, 0)

    def scatter(x_ref, dest_ref):
        def issue(r, carry):
            for k in range(TOP_K):
                pltpu.make_async_copy(
                    _row_tile(x_ref, r), _row_tile(out_hbm, dest_ref[TOP_K * r + k]),
                    sem).start()
            return carry

        lax.fori_loop(0, tb, issue, 0, unroll=8)
        for _ in range(TOP_K):
            pltpu.make_async_copy(
                x_ref, out_hbm.at[pl.ds(0, tb * SUBLANES)], sem).wait()

    @pl.when(i < steps_a)
    def _():
        scatter(xa_ref, dest_a_ref)

    @pl.when(i >= steps_a)
    def _():
        scatter(xb_ref, dest_b_ref)


def _dispatch(xa, dest_a, xb, dest_b, zeros, fill_starts, n_used, *, tb, n_tiles):
    na, nb = xa.shape[0] // SUBLANES, xb.shape[0] // SUBLANES
    tile_rows = zeros.shape[0] // SUBLANES
    assert na % tb == 0 and nb % tb == 0
    steps_a, steps_b = na // tb, nb // tb
    smem = pl.BlockSpec(memory_space=pltpu.SMEM)
    any_spec = pl.BlockSpec(memory_space=pl.ANY)
    body = functools.partial(
        _dispatch_body, tb=tb, steps_a=steps_a, n_fill=fill_starts.shape[0],
        n_tiles=n_tiles, tile_rows=tile_rows)
    return pl.pallas_call(
        body,
        grid=(steps_a + steps_b,),
        in_specs=[
            pl.BlockSpec((TOP_K * tb,), lambda i: (jnp.minimum(i, steps_a - 1),),
                         memory_space=pltpu.SMEM),
            pl.BlockSpec((TOP_K * tb,), lambda i: (jnp.maximum(i - steps_a, 0),),
                         memory_space=pltpu.SMEM),
            smem, smem,
            pl.BlockSpec((tb * SUBLANES, LANES), lambda i: (jnp.minimum(i, steps_a - 1), 0)),
            pl.BlockSpec((tb * SUBLANES, LANES), lambda i: (jnp.maximum(i - steps_a, 0), 0)),
            any_spec,
        ],
        out_specs=any_spec,
        out_shape=jax.ShapeDtypeStruct((n_tiles * tile_rows * SUBLANES, LANES), xa.dtype),
        scratch_shapes=[pltpu.SemaphoreType.DMA(())],
        compiler_params=_params(),
        name="moe_dispatch",
    )(dest_a, dest_b, fill_starts, n_used, xa, xb, zeros)


def _moe_body(tile_ref, te_ref, nu_ref, xs_ref, wg_ref, wu_ref, wd_ref, o_ref, xb_scr,
              h_scr, acc_scr, *, sub):
    del tile_ref, te_ref
    i = pl.program_id(0)
    f = pl.program_id(1)
    last_f = pl.num_programs(1) - 1
    tm = xb_scr.shape[0]

    @pl.when(i < nu_ref[0])
    def _():
        @pl.when(f == 0)
        def _():
            xb_scr[...] = _load_row_tiles(xs_ref, tm).astype(BF16)

        xb = xb_scr[...]
        for c in range(wg_ref.shape[2] // sub):
            cols = slice(c * sub, (c + 1) * sub)
            gt = _dot(xb, wg_ref[0, :, cols])
            up = _dot(xb, wu_ref[0, :, cols])
            h_scr[:, cols] = (gt * _sigmoid(gt) * up).astype(BF16)
        y = _dot(h_scr[...], wd_ref[0])

        @pl.when(f == 0)
        def _():
            acc_scr[...] = y

        @pl.when((f != 0) & (f != last_f))
        def _():
            acc_scr[...] += y

        @pl.when(f == last_f)
        def _():
            _store_row_tiles(o_ref, acc_scr[...] + y)

    @pl.when((i >= nu_ref[0]) & (f == 0))
    def _():
        o_ref[...] = jnp.zeros(o_ref.shape, o_ref.dtype)


def _moe_grouped(xs, tile_idx, tile_expert, n_used, wg, wu, wd, *, tm, f_chunk, sub):
    r = xs.shape[0] // SUBLANES
    d, ff = wg.shape[1], wg.shape[2]
    assert r % tm == 0 and ff % f_chunk == 0 and f_chunk % sub == 0 and ff // f_chunk >= 2
    grid_spec = pltpu.PrefetchScalarGridSpec(
        num_scalar_prefetch=3,
        grid=(r // tm, ff // f_chunk),
        in_specs=[
            pl.BlockSpec((tm * SUBLANES, LANES), lambda i, f, ti, te, nu: (ti[i], 0)),
            pl.BlockSpec((1, d, f_chunk), lambda i, f, ti, te, nu: (te[i], 0, f)),
            pl.BlockSpec((1, d, f_chunk), lambda i, f, ti, te, nu: (te[i], 0, f)),
            pl.BlockSpec((1, f_chunk, d), lambda i, f, ti, te, nu: (te[i], f, 0)),
        ],
        out_specs=pl.BlockSpec((tm * SUBLANES, LANES), lambda i, f, ti, te, nu: (i, 0)),
        scratch_shapes=[pltpu.VMEM((tm, d), BF16), pltpu.VMEM((tm, f_chunk), BF16),
                        pltpu.VMEM((tm, d), F32)],
    )
    return pl.pallas_call(
        functools.partial(_moe_body, sub=sub),
        grid_spec=grid_spec,
        out_shape=jax.ShapeDtypeStruct(xs.shape, F32),
        compiler_params=_params(2),
        name="moe_grouped",
    )(tile_idx, tile_expert, n_used, xs, wg, wu, wd)


def _combine_body(dest_ref, x_ref, rg_ref, ys_hbm, g_ref, b_ref, o_ref, y_scr, sem, *,
                  tb, alpha):
    def issue(r, carry):
        for k in range(TOP_K):
            pltpu.make_async_copy(
                _row_tile(ys_hbm, dest_ref[TOP_K * r + k]), _row_tile(y_scr.at[k], r),
                sem).start()
        return carry

    lax.fori_loop(0, tb, issue, 0, unroll=8)
    for k in range(TOP_K):
        pltpu.make_async_copy(
            ys_hbm.at[pl.ds(0, tb * SUBLANES)], y_scr.at[k], sem).wait()

    rg = rg_ref[...]
    moe = (rg[:, 0:1] * _load_row_tiles(y_scr.at[0], tb)
           + rg[:, 1:2] * _load_row_tiles(y_scr.at[1], tb))
    x = _load_row_tiles(x_ref, tb)
    o_ref[...] = _ln(alpha * x + moe, g_ref[...], b_ref[...])


def _combine(x, rg, dest, ys, g, b, *, tb, alpha):
    n, d = x.shape[0] // SUBLANES, SUBLANES * LANES
    assert n % tb == 0
    body = functools.partial(_combine_body, tb=tb, alpha=alpha)
    return pl.pallas_call(
        body,
        grid=(n // tb,),
        in_specs=[
            pl.BlockSpec((TOP_K * tb,), lambda i: (i,), memory_space=pltpu.SMEM),
            pl.BlockSpec((tb * SUBLANES, LANES), lambda i: (i, 0)),
            pl.BlockSpec((tb, LANES), lambda i: (i, 0)),
            pl.BlockSpec(memory_space=pl.ANY),
            _full(g.shape), _full(b.shape),
        ],
        out_specs=pl.BlockSpec((tb, d), lambda i: (i, 0)),
        out_shape=jax.ShapeDtypeStruct((n, d), F32),
        scratch_shapes=[pltpu.VMEM((TOP_K, tb * SUBLANES, LANES), F32),
                        pltpu.SemaphoreType.DMA(())],
        compiler_params=_params(),
        name="moe_combine",
    )(dest, x, rg, ys, g, b)


MOE_TILE = 1024
MOE_F_CHUNK = 1792
MOE_F_SUB = 256


def _row(v):
    return v.reshape(1, -1).astype(F32)


def kernel(x_prompt, x_sample, cache_conv, gm_w_in, gm_b_in, gm_lnv_g, gm_lnv_b, gm_w_s, gm_b_s, gm_w_out, gm_b_out, cv_w_pw1, cv_b_pw1, cv_w_dw, cv_b_dw, cv_ln_g, cv_ln_b, cv_w_pw2, cv_b_pw2, ff_w_gate, ff_w_up, ff_w_down, moe_w_router, moe_b_router, moe_w_gate, moe_w_up, moe_w_down, ln_g, ln_b):
    bp, sp, d = x_prompt.shape
    bs, ss, _ = x_sample.shape
    depth = ln_g.shape[0]
    assert depth == 2, "one gMLP layer followed by one conv/MoE layer"
    alpha = (2.0 * depth) ** 0.25
    n_experts = moe_w_router.shape[-1]
    width = cv_w_dw.shape[1]
    groups = gm_w_s.shape[1]
    np_, ns_ = bp * sp, bs * ss
    xp = x_prompt.reshape(np_, d)
    xs_ = x_sample.reshape(ns_, d)

    w_s = gm_w_s[0]
    pos = jnp.arange(GMLP_CHUNK) // CHUNK
    mask = pos[None, :] <= pos[:, None]
    wmix_p = jnp.where(mask[None], w_s, 0.0).astype(BF16)
    bmix_p = gm_b_s[0][:, :, None].astype(F32)
    assert GMLP_CHUNK % ss == 0 and ss <= CHUNK
    rep = GMLP_CHUNK // ss
    blk = w_s[:, :ss, :ss]
    eye = jnp.eye(rep, dtype=F32)
    wmix_s = jnp.einsum("ab,gij->gaibj", eye, blk).reshape(
        groups, GMLP_CHUNK, GMLP_CHUNK).astype(BF16)
    bmix_s = jnp.tile(gm_b_s[0][:, :ss], (1, rep))[:, :, None].astype(F32)

    gm_common = dict(alpha=alpha)
    gm_w = (gm_w_in[0].astype(BF16), _row(gm_b_in[0]), _row(gm_lnv_g[0]),
            _row(gm_lnv_b[0]))
    gm_tail = (gm_w_out[0].astype(BF16), _row(gm_b_out[0]), _row(ln_g[0, 0]),
               _row(ln_b[0, 0]))
    xp, vp = _gmlp_layer(xp, *gm_w, wmix_p, bmix_p, *gm_tail, tm=512, seq_len=sp,
                         v_rows=GMLP_CHUNK, **gm_common)
    xs_, vs = _gmlp_layer(xs_, *gm_w, wmix_s, bmix_s, *gm_tail, tm=256, seq_len=ss,
                          v_rows=None, **gm_common)
    gm_state_p = vp.reshape(1, bp, GMLP_CHUNK, -1)
    gm_state_s = vs.reshape(1, bs, ss, -1)

    ff_w = (ff_w_gate[0].astype(BF16), ff_w_up[0].astype(BF16),
            ff_w_down[0].astype(BF16), _row(ln_g[0, 1]), _row(ln_b[0, 1]))
    xp = _ffn_layer(xp, *ff_w, tm=512, alpha=alpha)
    xs_ = _ffn_layer(xs_, *ff_w, tm=512, alpha=alpha)

    nb = d // LANES
    wdw = jnp.transpose(cv_w_dw[0].reshape(width, nb, LANES), (1, 0, 2)).astype(F32)
    wr_f32 = jnp.zeros((d, LANES), F32).at[:, :n_experts].set(moe_w_router[0])
    wr_hi = wr_f32.astype(BF16)
    wr = jnp.concatenate([wr_hi, (wr_f32 - wr_hi.astype(F32)).astype(BF16)], axis=1)
    br = jnp.zeros((1, LANES), F32).at[0, :n_experts].set(moe_b_router[0])
    cv_w = (cv_w_pw1[0].astype(BF16), _row(cv_b_pw1[0]), wdw, _row(cv_b_dw[0]),
            _row(cv_ln_g[0]), _row(cv_ln_b[0]), cv_w_pw2[0].astype(BF16),
            _row(cv_b_pw2[0]), _row(ln_g[1, 0]), _row(ln_b[1, 0]), wr, br)
    cv_common = dict(alpha=alpha, width=width, n_experts=n_experts)
    cnt0 = jnp.zeros((1, LANES), F32)
    xp, st_p, ri_p, rg_p, cnt_p = _conv_layer(
        xp, None, *cv_w, cnt0, ts=512, seq_len=sp, **cv_common)
    hist = width - 1
    cache = jnp.pad(cache_conv[0].astype(F32), ((0, 0), (HIST_ROWS - hist, 0), (0, 0)))
    xs_, st_s, ri_s, rg_s, cnt_s = _conv_layer(
        xs_, cache, *cv_w, cnt_p, ts=ss, seq_len=ss, **cv_common)
    conv_state_p = st_p[None, :, HIST_ROWS - hist:, :]
    conv_state_s = st_s[None, :, HIST_ROWS - hist:, :]

    tm = MOE_TILE
    n_assign = (np_ + ns_) * TOP_K
    n_tiles = (n_assign + n_experts * (tm - 1)) // tm
    counts = cnt_s[0, :n_experts].astype(jnp.int32)
    tiles_e = (counts + tm - 1) // tm
    tile_end = jnp.cumsum(tiles_e)
    offs = (tile_end - tiles_e) * tm
    n_used = tile_end[-1:]
    tile_idx = jnp.minimum(jnp.arange(n_tiles, dtype=jnp.int32), n_used[0] - 1)
    tile_expert = jnp.minimum(
        jnp.sum(tile_idx[:, None] >= tile_end[None, :], axis=1), n_experts - 1
    ).astype(jnp.int32)
    fill_starts = (jnp.maximum(tile_end, 1) - 1).astype(jnp.int32) * tm

    def dest_of(ri):
        e_idx = ri[:, 0:TOP_K]
        sel = e_idx[:, :, None] == jnp.arange(n_experts, dtype=jnp.int32)
        start = jnp.sum(jnp.where(sel, offs.astype(jnp.int32), 0), axis=-1)
        return (start + ri[:, TOP_K:2 * TOP_K]).reshape(-1).astype(jnp.int32)

    dest_p, dest_s = dest_of(ri_p), dest_of(ri_s)
    zeros = jnp.zeros((tm * SUBLANES, LANES), F32)
    n_used = n_used.astype(jnp.int32)
    xsort = _dispatch(xp, dest_p, xs_, dest_s, zeros, fill_starts, n_used, tb=1024,
                      n_tiles=n_tiles)
    ysort = _moe_grouped(
        xsort, tile_idx, tile_expert, n_used,
        moe_w_gate[0].astype(BF16), moe_w_up[0].astype(BF16),
        moe_w_down[0].astype(BF16), tm=tm, f_chunk=MOE_F_CHUNK, sub=MOE_F_SUB)
    fin = (_row(ln_g[1, 1]), _row(ln_b[1, 1]))
    yp = _combine(xp, rg_p, dest_p, ysort, *fin, tb=512, alpha=alpha)
    ys_ = _combine(xs_, rg_s, dest_s, ysort, *fin, tb=512, alpha=alpha)

    return (yp.reshape(bp, sp, d), ys_.reshape(bs, ss, d), gm_state_p, gm_state_s,
            conv_state_p, conv_state_s)
```

```python
import functools

import jax
import jax.numpy as jnp
from jax import lax
from jax.experimental import pallas as pl
from jax.experimental.pallas import tpu as pltpu

F32 = jnp.float32
BF16 = jnp.bfloat16

LANES = 128
SUBLANES = 8
LN_EPS = 1e-5
CHUNK = 64
GMLP_CHUNK = 128
TOP_K = 2
DMA_PRIORITIES = 2
VMEM_LIMIT = 56 * 1024 * 1024


def _ln(z, g, b):
    mu = jnp.mean(z, axis=-1, keepdims=True)
    zc = z - mu
    var = jnp.mean(zc * zc, axis=-1, keepdims=True)
    return zc * lax.rsqrt(var + LN_EPS) * g + b


def _gelu(x):
    return 0.5 * x * (1.0 + lax.erf(x * (0.5 ** 0.5)))


def _sigmoid(x):
    return 1.0 / (1.0 + jnp.exp(-x))


def _dot(a, b):
    return jnp.dot(a, b, preferred_element_type=F32)


def _load_row_tiles(ref, rows):
    return jnp.concatenate(
        [ref[pl.ds(j, rows, stride=SUBLANES), :] for j in range(SUBLANES)], axis=1)


def _row_tile(ref, r):
    return ref.at[pl.ds(pl.multiple_of(r * SUBLANES, SUBLANES), SUBLANES)]


def _store_row_tiles(ref, value):
    rows = value.shape[0]
    for j in range(SUBLANES):
        ref[pl.ds(j, rows, stride=SUBLANES), :] = value[:, j * LANES:(j + 1) * LANES]


def _full(shape):
    zeros = (0,) * len(shape)
    return pl.BlockSpec(shape, lambda *_: zeros)


def _params(n_axes=1):
    return pltpu.CompilerParams(
        dimension_semantics=("arbitrary",) * n_axes, vmem_limit_bytes=VMEM_LIMIT)


def _gmlp_body(x_ref, w_in_ref, b_in_ref, lng_ref, lnb_ref, wmix_ref, bmix_ref,
               w_out_ref, b_out_ref, g_ref, b_ref, o_ref, v_ref,
               v_scr, vn_scr, gated_scr, *, tm, dv, groups, alpha, v_rows,
               col_chunk):
    gd = dv // groups
    xb = x_ref[...].astype(BF16)
    n_cc = dv // col_chunk

    s1 = jnp.zeros((tm, 1), F32)
    for c in range(n_cc):
        cols = slice(dv + c * col_chunk, dv + (c + 1) * col_chunk)
        h = _gelu(_dot(xb, w_in_ref[:, cols]) + b_in_ref[:, cols])
        v_scr[:, c * col_chunk:(c + 1) * col_chunk] = h
        s1 = s1 + jnp.sum(h, axis=-1, keepdims=True)
    mu = s1 * (1.0 / dv)
    s2 = jnp.zeros((tm, 1), F32)
    for c in range(n_cc):
        d = v_scr[:, c * col_chunk:(c + 1) * col_chunk] - mu
        s2 = s2 + jnp.sum(d * d, axis=-1, keepdims=True)
    rstd = lax.rsqrt(s2 * (1.0 / dv) + LN_EPS)
    keep = tm if v_rows is None else v_rows
    for c in range(n_cc):
        cols = slice(c * col_chunk, (c + 1) * col_chunk)
        vn = (v_scr[:, cols] - mu) * rstd * lng_ref[:, cols] + lnb_ref[:, cols]
        v_ref[:, cols] = vn[tm - keep:, :]
        vn_scr[:, cols] = vn.astype(BF16)

    n_ch = tm // GMLP_CHUNK
    for g in range(groups):
        cols = slice(g * gd, (g + 1) * gd)
        u = _gelu(_dot(xb, w_in_ref[:, cols]) + b_in_ref[:, cols])
        vg = jnp.concatenate(
            [vn_scr[c * GMLP_CHUNK:(c + 1) * GMLP_CHUNK, cols] for c in range(n_ch)], axis=1)
        mixed = _dot(wmix_ref[g], vg) + bmix_ref[g]
        for c in range(n_ch):
            rows = slice(c * GMLP_CHUNK, (c + 1) * GMLP_CHUNK)
            gated_scr[rows, cols] = (u[rows, :] * mixed[:, c * gd:(c + 1) * gd]).astype(BF16)

    y = _dot(gated_scr[...], w_out_ref[...]) + b_out_ref[...]
    o_ref[...] = _ln(alpha * x_ref[...] + y, g_ref[...], b_ref[...])


def _gmlp_layer(x, w_in, b_in, lng, lnb, wmix, bmix, w_out, b_out, g, b, *,
                tm, alpha, seq_len, v_rows, col_chunk=768):
    n, d = x.shape
    dv = w_out.shape[0]
    groups = wmix.shape[0]
    assert n % tm == 0 and tm % GMLP_CHUNK == 0
    if v_rows is None:
        tiles_per_seq = None
        v_shape = (n, dv)
        v_spec = pl.BlockSpec((tm, dv), lambda i: (i, 0))
    else:
        assert seq_len % tm == 0 and v_rows <= tm
        tiles_per_seq = seq_len // tm
        v_shape = (n // seq_len * v_rows, dv)
        v_spec = pl.BlockSpec((v_rows, dv), lambda i: (i // tiles_per_seq, 0))
    body = functools.partial(
        _gmlp_body, tm=tm, dv=dv, groups=groups, alpha=alpha, v_rows=v_rows,
        col_chunk=col_chunk)
    return pl.pallas_call(
        body,
        grid=(n // tm,),
        in_specs=[
            pl.BlockSpec((tm, d), lambda i: (i, 0)),
            _full(w_in.shape), _full(b_in.shape), _full(lng.shape), _full(lnb.shape),
            _full(wmix.shape), _full(bmix.shape), _full(w_out.shape),
            _full(b_out.shape), _full(g.shape), _full(b.shape),
        ],
        out_specs=[pl.BlockSpec((tm, d), lambda i: (i, 0)), v_spec],
        out_shape=[jax.ShapeDtypeStruct((n, d), F32),
                   jax.ShapeDtypeStruct(v_shape, F32)],
        scratch_shapes=[pltpu.VMEM((tm, dv), F32), pltpu.VMEM((tm, dv), BF16),
                        pltpu.VMEM((tm, dv), BF16)],
        compiler_params=_params(),
        name="gmlp_layer",
    )(x, w_in, b_in, lng, lnb, wmix, bmix, w_out, b_out, g, b)


def _ffn_body(x_ref, wg_ref, wu_ref, wd_ref, g_ref, b_ref, o_ref, h_scr, *,
              alpha, f_chunk):
    xb = x_ref[...].astype(BF16)
    ff = wg_ref.shape[1]
    for c in range(ff // f_chunk):
        cols = slice(c * f_chunk, (c + 1) * f_chunk)
        gt = _dot(xb, wg_ref[:, cols])
        up = _dot(xb, wu_ref[:, cols])
        h_scr[:, cols] = (gt * _sigmoid(gt) * up).astype(BF16)
    y = _dot(h_scr[...], wd_ref[...])
    o_ref[...] = _ln(alpha * x_ref[...] + y, g_ref[...], b_ref[...])


def _ffn_layer(x, wg, wu, wd, g, b, *, tm, alpha):
    n, d = x.shape
    ff = wg.shape[1]
    assert n % tm == 0
    body = functools.partial(_ffn_body, alpha=alpha, f_chunk=512)
    return pl.pallas_call(
        body,
        grid=(n // tm,),
        in_specs=[pl.BlockSpec((tm, d), lambda i: (i, 0)), _full(wg.shape),
                  _full(wu.shape), _full(wd.shape), _full(g.shape), _full(b.shape)],
        out_specs=pl.BlockSpec((tm, d), lambda i: (i, 0)),
        out_shape=jax.ShapeDtypeStruct((n, d), F32),
        scratch_shapes=[pltpu.VMEM((tm, ff), BF16)],
        compiler_params=_params(),
        name="ffn_layer",
    )(x, wg, wu, wd, g, b)


HIST_ROWS = 32


def _conv_body(*refs, ts, seq_tiles, use_cache, alpha, width, n_experts, t_chunk):
    if use_cache:
        (x_ref, cache_ref, w1_ref, b1_ref, wdw_ref, bdw_ref, cg_ref, cb_ref, w2_ref,
         b2_ref, g_ref, b_ref, wr_ref, br_ref, cnt_in_ref,
         o_ref, state_ref, ri_ref, rg_ref, cnt_out_ref,
         rows_scr, act_scr, cnt_scr) = refs
    else:
        (x_ref, w1_ref, b1_ref, wdw_ref, bdw_ref, cg_ref, cb_ref, w2_ref,
         b2_ref, g_ref, b_ref, wr_ref, br_ref, cnt_in_ref,
         o_ref, state_ref, ri_ref, rg_ref, cnt_out_ref,
         rows_scr, act_scr, cnt_scr) = refs
        cache_ref = None
    d = x_ref.shape[1]
    nb = d // LANES
    hist = width - 1
    first = HIST_ROWS - hist
    i = pl.program_id(0)

    @pl.when(i == 0)
    def _():
        cnt_scr[...] = cnt_in_ref[...]

    if use_cache:
        for j in range(nb):
            rows_scr[j, 0:HIST_ROWS, :] = cache_ref[0, :, j * LANES:(j + 1) * LANES]
    else:
        @pl.when(i % seq_tiles == 0)
        def _():
            rows_scr[:, 0:HIST_ROWS, :] = jnp.zeros((nb, HIST_ROWS, LANES), F32)

        @pl.when(i % seq_tiles != 0)
        def _():
            rows_scr[:, 0:HIST_ROWS, :] = rows_scr[:, ts:ts + HIST_ROWS, :]

    xb = x_ref[...].astype(BF16)
    cw = 2 * LANES
    for c in range(d // cw):
        a = _dot(xb, w1_ref[:, c * cw:(c + 1) * cw]) + b1_ref[:, c * cw:(c + 1) * cw]
        gt = (_dot(xb, w1_ref[:, d + c * cw:d + (c + 1) * cw])
              + b1_ref[:, d + c * cw:d + (c + 1) * cw])
        glu = a * _sigmoid(gt)
        rows_scr[2 * c, HIST_ROWS:HIST_ROWS + ts, :] = glu[:, :LANES]
        rows_scr[2 * c + 1, HIST_ROWS:HIST_ROWS + ts, :] = glu[:, LANES:]

    for j in range(nb):
        state_ref[0, :, j * LANES:(j + 1) * LANES] = rows_scr[j, ts:ts + HIST_ROWS, :]

    def conv_chunk(tc, carry):
        base = tc * t_chunk if isinstance(tc, int) else pl.multiple_of(tc * t_chunk, t_chunk)
        ys = []
        s1 = jnp.zeros((t_chunk, 1), F32)
        for j in range(nb):
            acc = jnp.zeros((t_chunk, LANES), F32)
            for k in range(width):
                tap = wdw_ref[j, k:k + 1, :]
                acc = acc + rows_scr[j, pl.ds(base + first + k, t_chunk), :] * tap
            acc = acc + bdw_ref[:, j * LANES:(j + 1) * LANES]
            ys.append(acc)
            s1 = s1 + jnp.sum(acc, axis=-1, keepdims=True)
        mu = s1 * (1.0 / d)
        s2 = jnp.zeros((t_chunk, 1), F32)
        for j in range(nb):
            dj = ys[j] - mu
            s2 = s2 + jnp.sum(dj * dj, axis=-1, keepdims=True)
        rstd = lax.rsqrt(s2 * (1.0 / d) + LN_EPS)
        for j in range(nb):
            cols = slice(j * LANES, (j + 1) * LANES)
            yn = (ys[j] - mu) * rstd * cg_ref[:, cols] + cb_ref[:, cols]
            act_scr[pl.ds(base, t_chunk), cols] = (yn * _sigmoid(yn)).astype(BF16)
        return carry

    for tc in range(ts // t_chunk):
        conv_chunk(tc, 0)

    m = _dot(act_scr[...], w2_ref[...]) + b2_ref[...]
    x3 = _ln(alpha * x_ref[...] + m, g_ref[...], b_ref[...])
    _store_row_tiles(o_ref, x3)

    x3_hi = x3.astype(BF16)
    x3_lo = (x3 - x3_hi.astype(F32)).astype(BF16)
    hi_prod = _dot(x3_hi, wr_ref[...])
    logits = (hi_prod[:, :LANES] + (hi_prod[:, LANES:] + _dot(x3_lo, wr_ref[:, :LANES]))
              + br_ref[...])
    lane = lax.broadcasted_iota(jnp.int32, (ts, LANES), 1)
    neg = jnp.float32(-jnp.inf)
    logits = jnp.where(lane < n_experts, logits, neg)
    m1 = jnp.max(logits, axis=-1, keepdims=True)
    i1 = jnp.min(jnp.where(logits == m1, lane, LANES), axis=-1, keepdims=True)
    rest = jnp.where(lane == i1, neg, logits)
    m2 = jnp.max(rest, axis=-1, keepdims=True)
    i2 = jnp.min(jnp.where(rest == m2, lane, LANES), axis=-1, keepdims=True)
    e2 = jnp.exp(m2 - m1)
    den = 1.0 + e2
    g1 = 1.0 / den
    g2 = e2 / den
    sel1 = lane == i1
    sel2 = lane == i2
    onehot = jnp.where(sel1 | sel2, 1.0, 0.0)
    r_io = lax.broadcasted_iota(jnp.int32, (ts, ts), 0)
    c_io = lax.broadcasted_iota(jnp.int32, (ts, ts), 1)
    tri = jnp.where(c_io < r_io, 1.0, 0.0).astype(BF16)
    prefix = _dot(tri, onehot.astype(BF16)) + cnt_scr[...]
    p1 = jnp.sum(jnp.where(sel1, prefix, 0.0), axis=-1, keepdims=True).astype(jnp.int32)
    p2 = jnp.sum(jnp.where(sel2, prefix, 0.0), axis=-1, keepdims=True).astype(jnp.int32)
    cnt_new = cnt_scr[...] + jnp.sum(onehot, axis=0, keepdims=True)
    cnt_scr[...] = cnt_new
    cnt_out_ref[...] = cnt_new
    ri_ref[...] = jnp.where(lane == 0, i1, jnp.where(lane == 1, i2,
                  jnp.where(lane == 2, p1, jnp.where(lane == 3, p2, 0))))
    rg_ref[...] = jnp.where(lane == 0, g1, jnp.where(lane == 1, g2, 0.0))


def _conv_layer(x, cache, w1, b1, wdw, bdw, cg, cb, w2, b2, g, b, wr, br, cnt_in, *,
                ts, seq_len, alpha, width, n_experts):
    n, d = x.shape
    nb = d // LANES
    hist = HIST_ROWS
    assert seq_len % ts == 0 and width - 1 <= HIST_ROWS <= ts and nb == SUBLANES
    seq_tiles = seq_len // ts
    use_cache = cache is not None
    assert not use_cache or seq_tiles == 1
    t_chunk = min(ts, 128)
    body = functools.partial(
        _conv_body, ts=ts, seq_tiles=seq_tiles, use_cache=use_cache, alpha=alpha,
        width=width, n_experts=n_experts, t_chunk=t_chunk)
    weights = (w1, b1, wdw, bdw, cg, cb, w2, b2, g, b, wr, br, cnt_in)
    in_specs = [pl.BlockSpec((ts, d), lambda i: (i, 0))]
    args = [x]
    if use_cache:
        in_specs.append(pl.BlockSpec((1, hist, d), lambda i: (i, 0, 0)))
        args.append(cache)
    in_specs += [_full(w.shape) for w in weights]
    args += list(weights)
    n_seq = n // seq_len
    return pl.pallas_call(
        body,
        grid=(n // ts,),
        in_specs=in_specs,
        out_specs=[
            pl.BlockSpec((ts * nb, LANES), lambda i: (i, 0)),
            pl.BlockSpec((1, hist, d), lambda i: (i // seq_tiles, 0, 0)),
            pl.BlockSpec((ts, LANES), lambda i: (i, 0)),
            pl.BlockSpec((ts, LANES), lambda i: (i, 0)),
            _full((1, LANES)),
        ],
        out_shape=[
            jax.ShapeDtypeStruct((n * nb, LANES), F32),
            jax.ShapeDtypeStruct((n_seq, hist, d), F32),
            jax.ShapeDtypeStruct((n, LANES), jnp.int32),
            jax.ShapeDtypeStruct((n, LANES), F32),
            jax.ShapeDtypeStruct((1, LANES), F32),
        ],
        scratch_shapes=[pltpu.VMEM((nb, ts + HIST_ROWS, LANES), F32),
                        pltpu.VMEM((ts, d), BF16), pltpu.VMEM((1, LANES), F32)],
        compiler_params=_params(),
        name="conv_layer",
    )(*args)


def _dispatch_body(dest_a_ref, dest_b_ref, fill_ref, nu_ref, xa_ref, xb_ref,
                   out_hbm, zeros_scr, sem, *, tb, steps_a, n_fill, n_tiles, tile_rows):
    i = pl.program_id(0)

    def fill(start):
        start = pl.multiple_of(start * SUBLANES, tile_rows * SUBLANES)
        return pltpu.make_async_copy(
            zeros_scr, out_hbm.at[pl.ds(start, tile_rows * SUBLANES)], sem)

    @pl.when(i == 0)
    def _():
        zeros_scr[...] = jnp.zeros(zeros_scr.shape, zeros_scr.dtype)
        for e in range(n_fill):
            fill(fill_ref[e]).start()
        for e in range(n_fill):
            fill(fill_ref[e]).wait()

        def fill_unused(j, carry):
            fill(j * tile_rows).start()
            fill(j * tile_rows).wait()
            return carry

        lax.fori_loop(nu_ref[0], n_tiles, fill_unused, 0)

    def scatter(x_ref, dest_ref):
        def issue(r, carry):
            for k in range(TOP_K):
                pltpu.make_async_copy(
                    _row_tile(x_ref, r), _row_tile(out_hbm, dest_ref[TOP_K * r + k]),
                    sem).start(priority=k % DMA_PRIORITIES)
            return carry

        lax.fori_loop(0, tb, issue, 0, unroll=8)
        for _ in range(TOP_K):
            pltpu.make_async_copy(
                x_ref, out_hbm.at[pl.ds(0, tb * SUBLANES)], sem).wait()

    @pl.when(i < steps_a)
    def _():
        scatter(xa_ref, dest_a_ref)

    @pl.when(i >= steps_a)
    def _():
        scatter(xb_ref, dest_b_ref)


def _dispatch(xa, dest_a, xb, dest_b, fill_starts, n_used, *, tb, n_tiles, tile_rows):
    na, nb = xa.shape[0] // SUBLANES, xb.shape[0] // SUBLANES
    assert na % tb == 0 and nb % tb == 0
    steps_a, steps_b = na // tb, nb // tb
    smem = pl.BlockSpec(memory_space=pltpu.SMEM)
    any_spec = pl.BlockSpec(memory_space=pl.ANY)
    body = functools.partial(
        _dispatch_body, tb=tb, steps_a=steps_a, n_fill=fill_starts.shape[0],
        n_tiles=n_tiles, tile_rows=tile_rows)
    return pl.pallas_call(
        body,
        grid=(steps_a + steps_b,),
        in_specs=[
            pl.BlockSpec((TOP_K * tb,), lambda i: (jnp.minimum(i, steps_a - 1),),
                         memory_space=pltpu.SMEM),
            pl.BlockSpec((TOP_K * tb,), lambda i: (jnp.maximum(i - steps_a, 0),),
                         memory_space=pltpu.SMEM),
            smem, smem,
            pl.BlockSpec((tb * SUBLANES, LANES), lambda i: (jnp.minimum(i, steps_a - 1), 0)),
            pl.BlockSpec((tb * SUBLANES, LANES), lambda i: (jnp.maximum(i - steps_a, 0), 0)),
        ],
        out_specs=any_spec,
        out_shape=jax.ShapeDtypeStruct((n_tiles * tile_rows * SUBLANES, LANES), xa.dtype),
        scratch_shapes=[pltpu.VMEM((tile_rows * SUBLANES, LANES), xa.dtype),
                        pltpu.SemaphoreType.DMA(())],
        compiler_params=_params(),
        name="moe_dispatch",
    )(dest_a, dest_b, fill_starts, n_used, xa, xb)


def _moe_body(tile_ref, te_ref, nu_ref, xs_ref, wg_ref, wu_ref, wd_ref, o_ref, h_scr, *,
              sub):
    del tile_ref, te_ref
    i = pl.program_id(0)
    tm = h_scr.shape[0]

    @pl.when(i < nu_ref[0])
    def _():
        xb = _load_row_tiles(xs_ref, tm).astype(BF16)
        for c in range(wg_ref.shape[2] // sub):
            cols = slice(c * sub, (c + 1) * sub)
            gt = _dot(xb, wg_ref[0, :, cols])
            up = _dot(xb, wu_ref[0, :, cols])
            h_scr[:, cols] = (gt * _sigmoid(gt) * up).astype(BF16)
        _store_row_tiles(o_ref, _dot(h_scr[...], wd_ref[0]))

    @pl.when(i >= nu_ref[0])
    def _():
        o_ref[...] = jnp.zeros(o_ref.shape, o_ref.dtype)


def _moe_grouped(xs, tile_idx, tile_expert, n_used, wg, wu, wd, *, tm, sub):
    r = xs.shape[0] // SUBLANES
    d, ff = wg.shape[1], wg.shape[2]
    assert r % tm == 0 and ff % sub == 0
    once = pl.Buffered(1)
    grid_spec = pltpu.PrefetchScalarGridSpec(
        num_scalar_prefetch=3,
        grid=(r // tm,),
        in_specs=[
            pl.BlockSpec((tm * SUBLANES, LANES), lambda i, ti, te, nu: (ti[i], 0)),
            pl.BlockSpec((1, d, ff), lambda i, ti, te, nu: (te[i], 0, 0), pipeline_mode=once),
            pl.BlockSpec((1, d, ff), lambda i, ti, te, nu: (te[i], 0, 0), pipeline_mode=once),
            pl.BlockSpec((1, ff, d), lambda i, ti, te, nu: (te[i], 0, 0), pipeline_mode=once),
        ],
        out_specs=pl.BlockSpec((tm * SUBLANES, LANES), lambda i, ti, te, nu: (i, 0)),
        scratch_shapes=[pltpu.VMEM((tm, ff), BF16)],
    )
    return pl.pallas_call(
        functools.partial(_moe_body, sub=sub),
        grid_spec=grid_spec,
        out_shape=jax.ShapeDtypeStruct(xs.shape, F32),
        compiler_params=_params(),
        name="moe_grouped",
    )(tile_idx, tile_expert, n_used, xs, wg, wu, wd)


def _combine_body(dest_ref, x_ref, rg_ref, ys_hbm, g_ref, b_ref, o_ref, y_scr, sem, *,
                  tb, alpha):
    def issue(r, carry):
        for k in range(TOP_K):
            pltpu.make_async_copy(
                _row_tile(ys_hbm, dest_ref[TOP_K * r + k]), _row_tile(y_scr.at[k], r),
                sem).start(priority=k % DMA_PRIORITIES)
        return carry

    lax.fori_loop(0, tb, issue, 0, unroll=8)
    for k in range(TOP_K):
        pltpu.make_async_copy(
            ys_hbm.at[pl.ds(0, tb * SUBLANES)], y_scr.at[k], sem).wait()

    rg = rg_ref[...]
    moe = (rg[:, 0:1] * _load_row_tiles(y_scr.at[0], tb)
           + rg[:, 1:2] * _load_row_tiles(y_scr.at[1], tb))
    x = _load_row_tiles(x_ref, tb)
    o_ref[...] = _ln(alpha * x + moe, g_ref[...], b_ref[...])


def _combine(x, rg, dest, ys, g, b, *, tb, alpha):
    n, d = x.shape[0] // SUBLANES, SUBLANES * LANES
    assert n % tb == 0
    body = functools.partial(_combine_body, tb=tb, alpha=alpha)
    return pl.pallas_call(
        body,
        grid=(n // tb,),
        in_specs=[
            pl.BlockSpec((TOP_K * tb,), lambda i: (i,), memory_space=pltpu.SMEM),
            pl.BlockSpec((tb * SUBLANES, LANES), lambda i: (i, 0)),
            pl.BlockSpec((tb, LANES), lambda i: (i, 0)),
            pl.BlockSpec(memory_space=pl.ANY),
            _full(g.shape), _full(b.shape),
        ],
        out_specs=pl.BlockSpec((tb, d), lambda i: (i, 0)),
        out_shape=jax.ShapeDtypeStruct((n, d), F32),
        scratch_shapes=[pltpu.VMEM((TOP_K, tb * SUBLANES, LANES), F32),
                        pltpu.SemaphoreType.DMA(())],
        compiler_params=_params(),
        name="moe_combine",
    )(dest, x, rg, ys, g, b)


MOE_TILE = 1024
MOE_F_SUB = 256


def _row(v):
    return v.reshape(1, -1).astype(F32)


def kernel(x_prompt, x_sample, cache_conv, gm_w_in, gm_b_in, gm_lnv_g, gm_lnv_b, gm_w_s, gm_b_s, gm_w_out, gm_b_out, cv_w_pw1, cv_b_pw1, cv_w_dw, cv_b_dw, cv_ln_g, cv_ln_b, cv_w_pw2, cv_b_pw2, ff_w_gate, ff_w_up, ff_w_down, moe_w_router, moe_b_router, moe_w_gate, moe_w_up, moe_w_down, ln_g, ln_b):
    bp, sp, d = x_prompt.shape
    bs, ss, _ = x_sample.shape
    depth = ln_g.shape[0]
    assert depth == 2, "one gMLP layer followed by one conv/MoE layer"
    alpha = (2.0 * depth) ** 0.25
    n_experts = moe_w_router.shape[-1]
    width = cv_w_dw.shape[1]
    groups = gm_w_s.shape[1]
    np_, ns_ = bp * sp, bs * ss
    xp = x_prompt.reshape(np_, d)
    xs_ = x_sample.reshape(ns_, d)

    w_s = gm_w_s[0]
    pos = jnp.arange(GMLP_CHUNK) // CHUNK
    mask = pos[None, :] <= pos[:, None]
    wmix_p = jnp.where(mask[None], w_s, 0.0).astype(BF16)
    bmix_p = gm_b_s[0][:, :, None].astype(F32)
    assert GMLP_CHUNK % ss == 0 and ss <= CHUNK
    rep = GMLP_CHUNK // ss
    blk = w_s[:, :ss, :ss]
    eye = jnp.eye(rep, dtype=F32)
    wmix_s = jnp.einsum("ab,gij->gaibj", eye, blk).reshape(
        groups, GMLP_CHUNK, GMLP_CHUNK).astype(BF16)
    bmix_s = jnp.tile(gm_b_s[0][:, :ss], (1, rep))[:, :, None].astype(F32)

    gm_common = dict(alpha=alpha)
    gm_w = (gm_w_in[0].astype(BF16), _row(gm_b_in[0]), _row(gm_lnv_g[0]),
            _row(gm_lnv_b[0]))
    gm_tail = (gm_w_out[0].astype(BF16), _row(gm_b_out[0]), _row(ln_g[0, 0]),
               _row(ln_b[0, 0]))
    xp, vp = _gmlp_layer(xp, *gm_w, wmix_p, bmix_p, *gm_tail, tm=512, seq_len=sp,
                         v_rows=GMLP_CHUNK, **gm_common)
    xs_, vs = _gmlp_layer(xs_, *gm_w, wmix_s, bmix_s, *gm_tail, tm=256, seq_len=ss,
                          v_rows=None, **gm_common)
    gm_state_p = vp.reshape(1, bp, GMLP_CHUNK, -1)
    gm_state_s = vs.reshape(1, bs, ss, -1)

    ff_w = (ff_w_gate[0].astype(BF16), ff_w_up[0].astype(BF16),
            ff_w_down[0].astype(BF16), _row(ln_g[0, 1]), _row(ln_b[0, 1]))
    xp = _ffn_layer(xp, *ff_w, tm=512, alpha=alpha)
    xs_ = _ffn_layer(xs_, *ff_w, tm=512, alpha=alpha)

    nb = d // LANES
    wdw = jnp.transpose(cv_w_dw[0].reshape(width, nb, LANES), (1, 0, 2)).astype(F32)
    wr_f32 = jnp.zeros((d, LANES), F32).at[:, :n_experts].set(moe_w_router[0])
    wr_hi = wr_f32.astype(BF16)
    wr = jnp.concatenate([wr_hi, (wr_f32 - wr_hi.astype(F32)).astype(BF16)], axis=1)
    br = jnp.zeros((1, LANES), F32).at[0, :n_experts].set(moe_b_router[0])
    cv_w = (cv_w_pw1[0].astype(BF16), _row(cv_b_pw1[0]), wdw, _row(cv_b_dw[0]),
            _row(cv_ln_g[0]), _row(cv_ln_b[0]), cv_w_pw2[0].astype(BF16),
            _row(cv_b_pw2[0]), _row(ln_g[1, 0]), _row(ln_b[1, 0]), wr, br)
    cv_common = dict(alpha=alpha, width=width, n_experts=n_experts)
    cnt0 = jnp.zeros((1, LANES), F32)
    xp, st_p, ri_p, rg_p, cnt_p = _conv_layer(
        xp, None, *cv_w, cnt0, ts=512, seq_len=sp, **cv_common)
    hist = width - 1
    cache = jnp.pad(cache_conv[0].astype(F32), ((0, 0), (HIST_ROWS - hist, 0), (0, 0)))
    xs_, st_s, ri_s, rg_s, cnt_s = _conv_layer(
        xs_, cache, *cv_w, cnt_p, ts=ss, seq_len=ss, **cv_common)
    conv_state_p = st_p[None, :, HIST_ROWS - hist:, :]
    conv_state_s = st_s[None, :, HIST_ROWS - hist:, :]

    tm = MOE_TILE
    n_assign = (np_ + ns_) * TOP_K
    n_tiles = (n_assign + n_experts * (tm - 1)) // tm
    counts = cnt_s[0, :n_experts].astype(jnp.int32)
    tiles_e = (counts + tm - 1) // tm
    tile_end = jnp.cumsum(tiles_e)
    offs = (tile_end - tiles_e) * tm
    n_used = tile_end[-1:]
    tile_idx = jnp.minimum(jnp.arange(n_tiles, dtype=jnp.int32), n_used[0] - 1)
    tile_expert = jnp.minimum(
        jnp.sum(tile_idx[:, None] >= tile_end[None, :], axis=1), n_experts - 1
    ).astype(jnp.int32)
    fill_starts = (jnp.maximum(tile_end, 1) - 1).astype(jnp.int32) * tm

    def dest_of(ri):
        e_idx = ri[:, 0:TOP_K]
        sel = e_idx[:, :, None] == jnp.arange(n_experts, dtype=jnp.int32)
        start = jnp.sum(jnp.where(sel, offs.astype(jnp.int32), 0), axis=-1)
        return (start + ri[:, TOP_K:2 * TOP_K]).reshape(-1).astype(jnp.int32)

    dest_p, dest_s = dest_of(ri_p), dest_of(ri_s)
    n_used = n_used.astype(jnp.int32)
    xsort = _dispatch(xp, dest_p, xs_, dest_s, fill_starts, n_used, tb=1024,
                      n_tiles=n_tiles, tile_rows=tm)
    ysort = _moe_grouped(
        xsort, tile_idx, tile_expert, n_used,
        moe_w_gate[0].astype(BF16), moe_w_up[0].astype(BF16),
        moe_w_down[0].astype(BF16), tm=tm, sub=MOE_F_SUB)
    fin = (_row(ln_g[1, 1]), _row(ln_b[1, 1]))
    yp = _combine(xp, rg_p, dest_p, ysort, *fin, tb=512, alpha=alpha)
    ys_ = _combine(xs_, rg_s, dest_s, ysort, *fin, tb=512, alpha=alpha)

    return (yp.reshape(bp, sp, d), ys_.reshape(bs, ss, d), gm_state_p, gm_state_s,
            conv_state_p, conv_state_s)
```

```python
import functools

import jax
import jax.numpy as jnp
from jax import lax
from jax.experimental import pallas as pl
from jax.experimental.pallas import tpu as pltpu

F32 = jnp.float32
BF16 = jnp.bfloat16

LANES = 128
SUBLANES = 8
LN_EPS = 1e-5
CHUNK = 64
GMLP_CHUNK = 128
TOP_K = 2
DMA_PRIORITIES = 2
VMEM_LIMIT = 56 * 1024 * 1024


def _ln(z, g, b):
    mu = jnp.mean(z, axis=-1, keepdims=True)
    zc = z - mu
    var = jnp.mean(zc * zc, axis=-1, keepdims=True)
    return zc * lax.rsqrt(var + LN_EPS) * g + b


def _gelu(x):
    return 0.5 * x * (1.0 + lax.erf(x * (0.5 ** 0.5)))


def _sigmoid(x):
    return 1.0 / (1.0 + jnp.exp(-x))


def _dot(a, b):
    return jnp.dot(a, b, preferred_element_type=F32)


def _load_row_tiles(ref, rows):
    return jnp.concatenate(
        [ref[pl.ds(j, rows, stride=SUBLANES), :] for j in range(SUBLANES)], axis=1)


def _row_tile(ref, r):
    return ref.at[pl.ds(pl.multiple_of(r * SUBLANES, SUBLANES), SUBLANES)]


def _store_row_tiles(ref, value):
    rows = value.shape[0]
    for j in range(SUBLANES):
        ref[pl.ds(j, rows, stride=SUBLANES), :] = value[:, j * LANES:(j + 1) * LANES]


def _full(shape):
    zeros = (0,) * len(shape)
    return pl.BlockSpec(shape, lambda *_: zeros)


def _params(n_axes=1):
    return pltpu.CompilerParams(
        dimension_semantics=("arbitrary",) * n_axes, vmem_limit_bytes=VMEM_LIMIT)


def _gmlp_body(x_ref, w_in_ref, b_in_ref, lng_ref, lnb_ref, wmix_ref, bmix_ref,
               w_out_ref, b_out_ref, g_ref, b_ref, o_ref, v_ref,
               v_scr, vn_scr, gated_scr, *, tm, dv, groups, alpha, v_rows,
               col_chunk):
    gd = dv // groups
    xb = x_ref[...].astype(BF16)
    n_cc = dv // col_chunk

    s1 = jnp.zeros((tm, 1), F32)
    for c in range(n_cc):
        cols = slice(dv + c * col_chunk, dv + (c + 1) * col_chunk)
        h = _gelu(_dot(xb, w_in_ref[:, cols]) + b_in_ref[:, cols])
        v_scr[:, c * col_chunk:(c + 1) * col_chunk] = h
        s1 = s1 + jnp.sum(h, axis=-1, keepdims=True)
    mu = s1 * (1.0 / dv)
    s2 = jnp.zeros((tm, 1), F32)
    for c in range(n_cc):
        d = v_scr[:, c * col_chunk:(c + 1) * col_chunk] - mu
        s2 = s2 + jnp.sum(d * d, axis=-1, keepdims=True)
    rstd = lax.rsqrt(s2 * (1.0 / dv) + LN_EPS)
    keep = tm if v_rows is None else v_rows
    for c in range(n_cc):
        cols = slice(c * col_chunk, (c + 1) * col_chunk)
        vn = (v_scr[:, cols] - mu) * rstd * lng_ref[:, cols] + lnb_ref[:, cols]
        v_ref[:, cols] = vn[tm - keep:, :]
        vn_scr[:, cols] = vn.astype(BF16)

    n_ch = tm // GMLP_CHUNK
    for g in range(groups):
        cols = slice(g * gd, (g + 1) * gd)
        u = _gelu(_dot(xb, w_in_ref[:, cols]) + b_in_ref[:, cols])
        vg = jnp.concatenate(
            [vn_scr[c * GMLP_CHUNK:(c + 1) * GMLP_CHUNK, cols] for c in range(n_ch)], axis=1)
        mixed = _dot(wmix_ref[g], vg) + bmix_ref[g]
        for c in range(n_ch):
            rows = slice(c * GMLP_CHUNK, (c + 1) * GMLP_CHUNK)
            gated_scr[rows, cols] = (u[rows, :] * mixed[:, c * gd:(c + 1) * gd]).astype(BF16)

    y = _dot(gated_scr[...], w_out_ref[...]) + b_out_ref[...]
    o_ref[...] = _ln(alpha * x_ref[...] + y, g_ref[...], b_ref[...])


def _gmlp_layer(x, w_in, b_in, lng, lnb, wmix, bmix, w_out, b_out, g, b, *,
                tm, alpha, seq_len, v_rows, col_chunk=768):
    n, d = x.shape
    dv = w_out.shape[0]
    groups = wmix.shape[0]
    assert n % tm == 0 and tm % GMLP_CHUNK == 0
    if v_rows is None:
        v_shape = (n, dv)
        v_spec = pl.BlockSpec((tm, dv), lambda i: (i, 0))
    else:
        assert seq_len % tm == 0 and v_rows <= tm
        tiles_per_seq = seq_len // tm
        v_shape = (n // seq_len * v_rows, dv)
        v_spec = pl.BlockSpec((v_rows, dv), lambda i: (i // tiles_per_seq, 0))
    body = functools.partial(
        _gmlp_body, tm=tm, dv=dv, groups=groups, alpha=alpha, v_rows=v_rows,
        col_chunk=col_chunk)
    return pl.pallas_call(
        body,
        grid=(n // tm,),
        in_specs=[
            pl.BlockSpec((tm, d), lambda i: (i, 0)),
            _full(w_in.shape), _full(b_in.shape), _full(lng.shape), _full(lnb.shape),
            _full(wmix.shape), _full(bmix.shape), _full(w_out.shape),
            _full(b_out.shape), _full(g.shape), _full(b.shape),
        ],
        out_specs=[pl.BlockSpec((tm, d), lambda i: (i, 0)), v_spec],
        out_shape=[jax.ShapeDtypeStruct((n, d), F32),
                   jax.ShapeDtypeStruct(v_shape, F32)],
        scratch_shapes=[pltpu.VMEM((tm, dv), F32), pltpu.VMEM((tm, dv), BF16),
                        pltpu.VMEM((tm, dv), BF16)],
        compiler_params=_params(),
        name="gmlp_layer",
    )(x, w_in, b_in, lng, lnb, wmix, bmix, w_out, b_out, g, b)


def _ffn_body(x_ref, wg_ref, wu_ref, wd_ref, g_ref, b_ref, o_ref, h_scr, *,
              alpha, f_chunk):
    xb = x_ref[...].astype(BF16)
    ff = wg_ref.shape[1]
    for c in range(ff // f_chunk):
        cols = slice(c * f_chunk, (c + 1) * f_chunk)
        gt = _dot(xb, wg_ref[:, cols])
        up = _dot(xb, wu_ref[:, cols])
        h_scr[:, cols] = (gt * _sigmoid(gt) * up).astype(BF16)
    y = _dot(h_scr[...], wd_ref[...])
    o_ref[...] = _ln(alpha * x_ref[...] + y, g_ref[...], b_ref[...])


def _ffn_layer(x, wg, wu, wd, g, b, *, tm, alpha):
    n, d = x.shape
    ff = wg.shape[1]
    assert n % tm == 0
    body = functools.partial(_ffn_body, alpha=alpha, f_chunk=512)
    return pl.pallas_call(
        body,
        grid=(n // tm,),
        in_specs=[pl.BlockSpec((tm, d), lambda i: (i, 0)), _full(wg.shape),
                  _full(wu.shape), _full(wd.shape), _full(g.shape), _full(b.shape)],
        out_specs=pl.BlockSpec((tm, d), lambda i: (i, 0)),
        out_shape=jax.ShapeDtypeStruct((n, d), F32),
        scratch_shapes=[pltpu.VMEM((tm, ff), BF16)],
        compiler_params=_params(),
        name="ffn_layer",
    )(x, wg, wu, wd, g, b)


HIST_ROWS = 32


def _conv_body(*refs, ts, seq_tiles, use_cache, alpha, width, n_experts, t_chunk):
    if use_cache:
        (x_ref, cache_ref, w1_ref, b1_ref, wdw_ref, bdw_ref, cg_ref, cb_ref, w2_ref,
         b2_ref, g_ref, b_ref, wr_ref, br_ref, cnt_in_ref,
         o_ref, state_ref, ri_ref, rg_ref, cnt_out_ref,
         rows_scr, act_scr, cnt_scr) = refs
    else:
        (x_ref, w1_ref, b1_ref, wdw_ref, bdw_ref, cg_ref, cb_ref, w2_ref,
         b2_ref, g_ref, b_ref, wr_ref, br_ref, cnt_in_ref,
         o_ref, state_ref, ri_ref, rg_ref, cnt_out_ref,
         rows_scr, act_scr, cnt_scr) = refs
        cache_ref = None
    d = x_ref.shape[1]
    nb = d // LANES
    first = HIST_ROWS - (width - 1)
    i = pl.program_id(0)

    @pl.when(i == 0)
    def _():
        cnt_scr[...] = cnt_in_ref[...]

    if use_cache:
        for j in range(nb):
            rows_scr[j, 0:HIST_ROWS, :] = cache_ref[0, :, j * LANES:(j + 1) * LANES]
    else:
        @pl.when(i % seq_tiles == 0)
        def _():
            rows_scr[:, 0:HIST_ROWS, :] = jnp.zeros((nb, HIST_ROWS, LANES), F32)

        @pl.when(i % seq_tiles != 0)
        def _():
            rows_scr[:, 0:HIST_ROWS, :] = rows_scr[:, ts:ts + HIST_ROWS, :]

    xb = x_ref[...].astype(BF16)
    cw = 2 * LANES
    for c in range(d // cw):
        a = _dot(xb, w1_ref[:, c * cw:(c + 1) * cw]) + b1_ref[:, c * cw:(c + 1) * cw]
        gt = (_dot(xb, w1_ref[:, d + c * cw:d + (c + 1) * cw])
              + b1_ref[:, d + c * cw:d + (c + 1) * cw])
        glu = a * _sigmoid(gt)
        rows_scr[2 * c, HIST_ROWS:HIST_ROWS + ts, :] = glu[:, :LANES]
        rows_scr[2 * c + 1, HIST_ROWS:HIST_ROWS + ts, :] = glu[:, LANES:]

    for j in range(nb):
        state_ref[0, :, j * LANES:(j + 1) * LANES] = rows_scr[j, ts:ts + HIST_ROWS, :]

    def conv_chunk(tc):
        base = tc * t_chunk
        ys = []
        s1 = jnp.zeros((t_chunk, 1), F32)
        for j in range(nb):
            acc = jnp.zeros((t_chunk, LANES), F32)
            for k in range(width):
                tap = wdw_ref[j, k:k + 1, :]
                acc = acc + rows_scr[j, base + first + k:base + first + k + t_chunk, :] * tap
            acc = acc + bdw_ref[:, j * LANES:(j + 1) * LANES]
            ys.append(acc)
            s1 = s1 + jnp.sum(acc, axis=-1, keepdims=True)
        mu = s1 * (1.0 / d)
        s2 = jnp.zeros((t_chunk, 1), F32)
        for j in range(nb):
            dj = ys[j] - mu
            s2 = s2 + jnp.sum(dj * dj, axis=-1, keepdims=True)
        rstd = lax.rsqrt(s2 * (1.0 / d) + LN_EPS)
        for j in range(nb):
            cols = slice(j * LANES, (j + 1) * LANES)
            yn = (ys[j] - mu) * rstd * cg_ref[:, cols] + cb_ref[:, cols]
            act_scr[base:base + t_chunk, cols] = (yn * _sigmoid(yn)).astype(BF16)

    for tc in range(ts // t_chunk):
        conv_chunk(tc)

    m = _dot(act_scr[...], w2_ref[...]) + b2_ref[...]
    x3 = _ln(alpha * x_ref[...] + m, g_ref[...], b_ref[...])
    _store_row_tiles(o_ref, x3)

    x3_hi = x3.astype(BF16)
    x3_lo = (x3 - x3_hi.astype(F32)).astype(BF16)
    hi_prod = _dot(x3_hi, wr_ref[...])
    logits = (hi_prod[:, :LANES] + (hi_prod[:, LANES:] + _dot(x3_lo, wr_ref[:, :LANES]))
              + br_ref[...])
    lane = lax.broadcasted_iota(jnp.int32, (ts, LANES), 1)
    neg = jnp.float32(-jnp.inf)
    logits = jnp.where(lane < n_experts, logits, neg)
    m1 = jnp.max(logits, axis=-1, keepdims=True)
    i1 = jnp.min(jnp.where(logits == m1, lane, LANES), axis=-1, keepdims=True)
    rest = jnp.where(lane == i1, neg, logits)
    m2 = jnp.max(rest, axis=-1, keepdims=True)
    i2 = jnp.min(jnp.where(rest == m2, lane, LANES), axis=-1, keepdims=True)
    e2 = jnp.exp(m2 - m1)
    den = 1.0 + e2
    g1 = 1.0 / den
    g2 = e2 / den
    sel1 = lane == i1
    sel2 = lane == i2
    onehot = jnp.where(sel1 | sel2, 1.0, 0.0)
    r_io = lax.broadcasted_iota(jnp.int32, (ts, ts), 0)
    c_io = lax.broadcasted_iota(jnp.int32, (ts, ts), 1)
    tri = jnp.where(c_io < r_io, 1.0, 0.0).astype(BF16)
    prefix = _dot(tri, onehot.astype(BF16)) + cnt_scr[...]
    p1 = jnp.sum(jnp.where(sel1, prefix, 0.0), axis=-1, keepdims=True).astype(jnp.int32)
    p2 = jnp.sum(jnp.where(sel2, prefix, 0.0), axis=-1, keepdims=True).astype(jnp.int32)
    cnt_new = cnt_scr[...] + jnp.sum(onehot, axis=0, keepdims=True)
    cnt_scr[...] = cnt_new
    cnt_out_ref[...] = cnt_new
    ri_ref[...] = jnp.where(lane == 0, i1, jnp.where(lane == 1, i2,
                  jnp.where(lane == 2, p1, jnp.where(lane == 3, p2, 0))))
    rg_ref[...] = jnp.where(lane == 0, g1, jnp.where(lane == 1, g2, 0.0))


def _conv_layer(x, cache, w1, b1, wdw, bdw, cg, cb, w2, b2, g, b, wr, br, cnt_in, *,
                ts, seq_len, alpha, width, n_experts):
    n, d = x.shape
    nb = d // LANES
    hist = HIST_ROWS
    assert seq_len % ts == 0 and width - 1 <= HIST_ROWS <= ts and nb == SUBLANES
    seq_tiles = seq_len // ts
    use_cache = cache is not None
    assert not use_cache or seq_tiles == 1
    t_chunk = min(ts, 128)
    body = functools.partial(
        _conv_body, ts=ts, seq_tiles=seq_tiles, use_cache=use_cache, alpha=alpha,
        width=width, n_experts=n_experts, t_chunk=t_chunk)
    weights = (w1, b1, wdw, bdw, cg, cb, w2, b2, g, b, wr, br, cnt_in)
    in_specs = [pl.BlockSpec((ts, d), lambda i: (i, 0))]
    args = [x]
    if use_cache:
        in_specs.append(pl.BlockSpec((1, hist, d), lambda i: (i, 0, 0)))
        args.append(cache)
    in_specs += [_full(w.shape) for w in weights]
    args += list(weights)
    n_seq = n // seq_len
    return pl.pallas_call(
        body,
        grid=(n // ts,),
        in_specs=in_specs,
        out_specs=[
            pl.BlockSpec((ts * nb, LANES), lambda i: (i, 0)),
            pl.BlockSpec((1, hist, d), lambda i: (i // seq_tiles, 0, 0)),
            pl.BlockSpec((ts, LANES), lambda i: (i, 0)),
            pl.BlockSpec((ts, LANES), lambda i: (i, 0)),
            _full((1, LANES)),
        ],
        out_shape=[
            jax.ShapeDtypeStruct((n * nb, LANES), F32),
            jax.ShapeDtypeStruct((n_seq, hist, d), F32),
            jax.ShapeDtypeStruct((n, LANES), jnp.int32),
            jax.ShapeDtypeStruct((n, LANES), F32),
            jax.ShapeDtypeStruct((1, LANES), F32),
        ],
        scratch_shapes=[pltpu.VMEM((nb, ts + HIST_ROWS, LANES), F32),
                        pltpu.VMEM((ts, d), BF16), pltpu.VMEM((1, LANES), F32)],
        compiler_params=_params(),
        name="conv_layer",
    )(*args)


def _dispatch_body(dest_a_ref, dest_b_ref, fill_ref, nu_ref, xa_ref, xb_ref,
                   out_hbm, zeros_scr, sem, *, tb, steps_a, n_fill, n_tiles, tile_rows):
    i = pl.program_id(0)

    def fill(start):
        start = pl.multiple_of(start * SUBLANES, tile_rows * SUBLANES)
        return pltpu.make_async_copy(
            zeros_scr, out_hbm.at[pl.ds(start, tile_rows * SUBLANES)], sem)

    @pl.when(i == 0)
    def _():
        zeros_scr[...] = jnp.zeros(zeros_scr.shape, zeros_scr.dtype)
        for e in range(n_fill):
            fill(fill_ref[e]).start()
        for e in range(n_fill):
            fill(fill_ref[e]).wait()

        def fill_unused(j, carry):
            fill(j * tile_rows).start()
            fill(j * tile_rows).wait()
            return carry

        lax.fori_loop(nu_ref[0], n_tiles, fill_unused, 0)

    def scatter(x_ref, dest_ref):
        def issue(r, carry):
            for k in range(TOP_K):
                pltpu.make_async_copy(
                    _row_tile(x_ref, r), _row_tile(out_hbm, dest_ref[TOP_K * r + k]),
                    sem).start(priority=k % DMA_PRIORITIES)
            return carry

        lax.fori_loop(0, tb, issue, 0, unroll=8)
        for _ in range(TOP_K):
            pltpu.make_async_copy(
                x_ref, out_hbm.at[pl.ds(0, tb * SUBLANES)], sem).wait()

    @pl.when(i < steps_a)
    def _():
        scatter(xa_ref, dest_a_ref)

    @pl.when(i >= steps_a)
    def _():
        scatter(xb_ref, dest_b_ref)


def _dispatch(xa, dest_a, xb, dest_b, fill_starts, n_used, *, tb, n_tiles, tile_rows):
    na, nb = xa.shape[0] // SUBLANES, xb.shape[0] // SUBLANES
    assert na % tb == 0 and nb % tb == 0
    steps_a, steps_b = na // tb, nb // tb
    smem = pl.BlockSpec(memory_space=pltpu.SMEM)
    any_spec = pl.BlockSpec(memory_space=pl.ANY)
    body = functools.partial(
        _dispatch_body, tb=tb, steps_a=steps_a, n_fill=fill_starts.shape[0],
        n_tiles=n_tiles, tile_rows=tile_rows)
    return pl.pallas_call(
        body,
        grid=(steps_a + steps_b,),
        in_specs=[
            pl.BlockSpec((TOP_K * tb,), lambda i: (jnp.minimum(i, steps_a - 1),),
                         memory_space=pltpu.SMEM),
            pl.BlockSpec((TOP_K * tb,), lambda i: (jnp.maximum(i - steps_a, 0),),
                         memory_space=pltpu.SMEM),
            smem, smem,
            pl.BlockSpec((tb * SUBLANES, LANES), lambda i: (jnp.minimum(i, steps_a - 1), 0)),
            pl.BlockSpec((tb * SUBLANES, LANES), lambda i: (jnp.maximum(i - steps_a, 0), 0)),
        ],
        out_specs=any_spec,
        out_shape=jax.ShapeDtypeStruct((n_tiles * tile_rows * SUBLANES, LANES), xa.dtype),
        scratch_shapes=[pltpu.VMEM((tile_rows * SUBLANES, LANES), xa.dtype),
                        pltpu.SemaphoreType.DMA(())],
        compiler_params=_params(),
        name="moe_dispatch",
    )(dest_a, dest_b, fill_starts, n_used, xa, xb)


def _moe_body(tile_ref, te_ref, nu_ref, xs_ref, wg_ref, wu_ref, wd_ref, o_ref, h_scr, *,
              sub):
    del tile_ref, te_ref
    i = pl.program_id(0)
    tm = h_scr.shape[0]

    @pl.when(i < nu_ref[0])
    def _():
        xb = _load_row_tiles(xs_ref, tm).astype(BF16)
        for c in range(wg_ref.shape[2] // sub):
            cols = slice(c * sub, (c + 1) * sub)
            gt = _dot(xb, wg_ref[0, :, cols])
            up = _dot(xb, wu_ref[0, :, cols])
            h_scr[:, cols] = (gt * _sigmoid(gt) * up).astype(BF16)
        _store_row_tiles(o_ref, _dot(h_scr[...], wd_ref[0]))

    @pl.when(i >= nu_ref[0])
    def _():
        o_ref[...] = jnp.zeros(o_ref.shape, o_ref.dtype)


def _moe_grouped(xs, tile_idx, tile_expert, n_used, wg, wu, wd, *, tm, sub):
    r = xs.shape[0] // SUBLANES
    d, ff = wg.shape[1], wg.shape[2]
    assert r % tm == 0 and ff % sub == 0
    once = pl.Buffered(1)
    grid_spec = pltpu.PrefetchScalarGridSpec(
        num_scalar_prefetch=3,
        grid=(r // tm,),
        in_specs=[
            pl.BlockSpec((tm * SUBLANES, LANES), lambda i, ti, te, nu: (ti[i], 0)),
            pl.BlockSpec((1, d, ff), lambda i, ti, te, nu: (te[i], 0, 0), pipeline_mode=once),
            pl.BlockSpec((1, d, ff), lambda i, ti, te, nu: (te[i], 0, 0), pipeline_mode=once),
            pl.BlockSpec((1, ff, d), lambda i, ti, te, nu: (te[i], 0, 0), pipeline_mode=once),
        ],
        out_specs=pl.BlockSpec((tm * SUBLANES, LANES), lambda i, ti, te, nu: (i, 0)),
        scratch_shapes=[pltpu.VMEM((tm, ff), BF16)],
    )
    return pl.pallas_call(
        functools.partial(_moe_body, sub=sub),
        grid_spec=grid_spec,
        out_shape=jax.ShapeDtypeStruct(xs.shape, F32),
        compiler_params=_params(),
        name="moe_grouped",
    )(tile_idx, tile_expert, n_used, xs, wg, wu, wd)


def _combine_body(dest_ref, dest_next_ref, x_ref, rg_ref, ys_hbm, g_ref, b_ref, o_ref,
                  y_scr, sems, *, tb, alpha):
    i = pl.program_id(0)
    slot = i % 2

    def gather(d_ref, s):
        def issue(r, carry):
            for k in range(TOP_K):
                pltpu.make_async_copy(
                    _row_tile(ys_hbm, d_ref[TOP_K * r + k]), _row_tile(y_scr.at[s, k], r),
                    sems.at[s]).start(priority=k % DMA_PRIORITIES)
            return carry

        lax.fori_loop(0, tb, issue, 0, unroll=8)

    @pl.when(i == 0)
    def _():
        gather(dest_ref, slot)

    @pl.when(i + 1 < pl.num_programs(0))
    def _():
        gather(dest_next_ref, 1 - slot)

    for k in range(TOP_K):
        pltpu.make_async_copy(
            ys_hbm.at[pl.ds(0, tb * SUBLANES)], y_scr.at[slot, k], sems.at[slot]).wait()

    rg = rg_ref[...]
    moe = (rg[:, 0:1] * _load_row_tiles(y_scr.at[slot, 0], tb)
           + rg[:, 1:2] * _load_row_tiles(y_scr.at[slot, 1], tb))
    x = _load_row_tiles(x_ref, tb)
    o_ref[...] = _ln(alpha * x + moe, g_ref[...], b_ref[...])


def _combine(x, rg, dest, ys, g, b, *, tb, alpha):
    n, d = x.shape[0] // SUBLANES, SUBLANES * LANES
    assert n % tb == 0
    steps = n // tb
    body = functools.partial(_combine_body, tb=tb, alpha=alpha)
    return pl.pallas_call(
        body,
        grid=(steps,),
        in_specs=[
            pl.BlockSpec((TOP_K * tb,), lambda i: (i,), memory_space=pltpu.SMEM),
            pl.BlockSpec((TOP_K * tb,), lambda i: (jnp.minimum(i + 1, steps - 1),),
                         memory_space=pltpu.SMEM),
            pl.BlockSpec((tb * SUBLANES, LANES), lambda i: (i, 0)),
            pl.BlockSpec((tb, LANES), lambda i: (i, 0)),
            pl.BlockSpec(memory_space=pl.ANY),
            _full(g.shape), _full(b.shape),
        ],
        out_specs=pl.BlockSpec((tb, d), lambda i: (i, 0)),
        out_shape=jax.ShapeDtypeStruct((n, d), F32),
        scratch_shapes=[pltpu.VMEM((2, TOP_K, tb * SUBLANES, LANES), F32),
                        pltpu.SemaphoreType.DMA((2,))],
        compiler_params=_params(),
        name="moe_combine",
    )(dest, dest, x, rg, ys, g, b)


MOE_TILE = 1024
MOE_F_SUB = 256


def _row(v):
    return v.reshape(1, -1).astype(F32)


def kernel(x_prompt, x_sample, cache_conv, gm_w_in, gm_b_in, gm_lnv_g, gm_lnv_b, gm_w_s, gm_b_s, gm_w_out, gm_b_out, cv_w_pw1, cv_b_pw1, cv_w_dw, cv_b_dw, cv_ln_g, cv_ln_b, cv_w_pw2, cv_b_pw2, ff_w_gate, ff_w_up, ff_w_down, moe_w_router, moe_b_router, moe_w_gate, moe_w_up, moe_w_down, ln_g, ln_b):
    bp, sp, d = x_prompt.shape
    bs, ss, _ = x_sample.shape
    depth = ln_g.shape[0]
    assert depth == 2, "one gMLP layer followed by one conv/MoE layer"
    alpha = (2.0 * depth) ** 0.25
    n_experts = moe_w_router.shape[-1]
    width = cv_w_dw.shape[1]
    groups = gm_w_s.shape[1]
    np_, ns_ = bp * sp, bs * ss
    xp = x_prompt.reshape(np_, d)
    xs_ = x_sample.reshape(ns_, d)

    w_s = gm_w_s[0]
    pos = jnp.arange(GMLP_CHUNK) // CHUNK
    mask = pos[None, :] <= pos[:, None]
    wmix_p = jnp.where(mask[None], w_s, 0.0).astype(BF16)
    bmix_p = gm_b_s[0][:, :, None].astype(F32)
    assert GMLP_CHUNK % ss == 0 and ss <= CHUNK
    rep = GMLP_CHUNK // ss
    blk = w_s[:, :ss, :ss]
    eye = jnp.eye(rep, dtype=F32)
    wmix_s = jnp.einsum("ab,gij->gaibj", eye, blk).reshape(
        groups, GMLP_CHUNK, GMLP_CHUNK).astype(BF16)
    bmix_s = jnp.tile(gm_b_s[0][:, :ss], (1, rep))[:, :, None].astype(F32)

    gm_common = dict(alpha=alpha)
    gm_w = (gm_w_in[0].astype(BF16), _row(gm_b_in[0]), _row(gm_lnv_g[0]),
            _row(gm_lnv_b[0]))
    gm_tail = (gm_w_out[0].astype(BF16), _row(gm_b_out[0]), _row(ln_g[0, 0]),
               _row(ln_b[0, 0]))
    xp, vp = _gmlp_layer(xp, *gm_w, wmix_p, bmix_p, *gm_tail, tm=512, seq_len=sp,
                         v_rows=GMLP_CHUNK, **gm_common)
    xs_, vs = _gmlp_layer(xs_, *gm_w, wmix_s, bmix_s, *gm_tail, tm=256, seq_len=ss,
                          v_rows=None, **gm_common)
    gm_state_p = vp.reshape(1, bp, GMLP_CHUNK, -1)
    gm_state_s = vs.reshape(1, bs, ss, -1)

    ff_w = (ff_w_gate[0].astype(BF16), ff_w_up[0].astype(BF16),
            ff_w_down[0].astype(BF16), _row(ln_g[0, 1]), _row(ln_b[0, 1]))
    xp = _ffn_layer(xp, *ff_w, tm=512, alpha=alpha)
    xs_ = _ffn_layer(xs_, *ff_w, tm=512, alpha=alpha)

    nb = d // LANES
    wdw = jnp.transpose(cv_w_dw[0].reshape(width, nb, LANES), (1, 0, 2)).astype(F32)
    wr_f32 = jnp.zeros((d, LANES), F32).at[:, :n_experts].set(moe_w_router[0])
    wr_hi = wr_f32.astype(BF16)
    wr = jnp.concatenate([wr_hi, (wr_f32 - wr_hi.astype(F32)).astype(BF16)], axis=1)
    br = jnp.zeros((1, LANES), F32).at[0, :n_experts].set(moe_b_router[0])
    cv_w = (cv_w_pw1[0].astype(BF16), _row(cv_b_pw1[0]), wdw, _row(cv_b_dw[0]),
            _row(cv_ln_g[0]), _row(cv_ln_b[0]), cv_w_pw2[0].astype(BF16),
            _row(cv_b_pw2[0]), _row(ln_g[1, 0]), _row(ln_b[1, 0]), wr, br)
    cv_common = dict(alpha=alpha, width=width, n_experts=n_experts)
    cnt0 = jnp.zeros((1, LANES), F32)
    xp, st_p, ri_p, rg_p, cnt_p = _conv_layer(
        xp, None, *cv_w, cnt0, ts=512, seq_len=sp, **cv_common)
    hist = width - 1
    cache = jnp.pad(cache_conv[0].astype(F32), ((0, 0), (HIST_ROWS - hist, 0), (0, 0)))
    xs_, st_s, ri_s, rg_s, cnt_s = _conv_layer(
        xs_, cache, *cv_w, cnt_p, ts=ss, seq_len=ss, **cv_common)
    conv_state_p = st_p[None, :, HIST_ROWS - hist:, :]
    conv_state_s = st_s[None, :, HIST_ROWS - hist:, :]

    tm = MOE_TILE
    n_assign = (np_ + ns_) * TOP_K
    n_tiles = (n_assign + n_experts * (tm - 1)) // tm
    counts = cnt_s[0, :n_experts].astype(jnp.int32)
    tiles_e = (counts + tm - 1) // tm
    tile_end = jnp.cumsum(tiles_e)
    offs = (tile_end - tiles_e) * tm
    n_used = tile_end[-1:]
    tile_idx = jnp.minimum(jnp.arange(n_tiles, dtype=jnp.int32), n_used[0] - 1)
    tile_expert = jnp.minimum(
        jnp.sum(tile_idx[:, None] >= tile_end[None, :], axis=1), n_experts - 1
    ).astype(jnp.int32)
    fill_starts = (jnp.maximum(tile_end, 1) - 1).astype(jnp.int32) * tm

    def dest_of(ri):
        e_idx = ri[:, 0:TOP_K]
        sel = e_idx[:, :, None] == jnp.arange(n_experts, dtype=jnp.int32)
        start = jnp.sum(jnp.where(sel, offs.astype(jnp.int32), 0), axis=-1)
        return (start + ri[:, TOP_K:2 * TOP_K]).reshape(-1).astype(jnp.int32)

    dest_p, dest_s = dest_of(ri_p), dest_of(ri_s)
    n_used = n_used.astype(jnp.int32)
    xsort = _dispatch(xp, dest_p, xs_, dest_s, fill_starts, n_used, tb=1024,
                      n_tiles=n_tiles, tile_rows=tm)
    ysort = _moe_grouped(
        xsort, tile_idx, tile_expert, n_used,
        moe_w_gate[0].astype(BF16), moe_w_up[0].astype(BF16),
        moe_w_down[0].astype(BF16), tm=tm, sub=MOE_F_SUB)
    fin = (_row(ln_g[1, 1]), _row(ln_b[1, 1]))
    yp = _combine(xp, rg_p, dest_p, ysort, *fin, tb=512, alpha=alpha)
    ys_ = _combine(xs_, rg_s, dest_s, ysort, *fin, tb=512, alpha=alpha)

    return (yp.reshape(bp, sp, d), ys_.reshape(bs, ss, d), gm_state_p, gm_state_s,
            conv_state_p, conv_state_s)
```

```python
import functools

import jax
import jax.numpy as jnp
from jax import lax
from jax.experimental import pallas as pl
from jax.experimental.pallas import tpu as pltpu

F32 = jnp.float32
BF16 = jnp.bfloat16

LANES = 128
SUBLANES = 8
LN_EPS = 1e-5
CHUNK = 64
GMLP_CHUNK = 128
TOP_K = 2
DMA_PRIORITIES = 2
VMEM_LIMIT = 56 * 1024 * 1024


def _ln(z, g, b):
    mu = jnp.mean(z, axis=-1, keepdims=True)
    zc = z - mu
    var = jnp.mean(zc * zc, axis=-1, keepdims=True)
    return zc * lax.rsqrt(var + LN_EPS) * g + b


def _gelu(x):
    return 0.5 * x * (1.0 + lax.erf(x * (0.5 ** 0.5)))


def _sigmoid(x):
    return 1.0 / (1.0 + jnp.exp(-x))


def _dot(a, b):
    return jnp.dot(a, b, preferred_element_type=F32)


def _load_row_tiles(ref, rows):
    return jnp.concatenate(
        [ref[pl.ds(j, rows, stride=SUBLANES), :] for j in range(SUBLANES)], axis=1)


def _row_tile(ref, r):
    return ref.at[pl.ds(pl.multiple_of(r * SUBLANES, SUBLANES), SUBLANES)]


def _store_row_tiles(ref, value):
    rows = value.shape[0]
    for j in range(SUBLANES):
        ref[pl.ds(j, rows, stride=SUBLANES), :] = value[:, j * LANES:(j + 1) * LANES]


def _full(shape):
    zeros = (0,) * len(shape)
    return pl.BlockSpec(shape, lambda *_: zeros)


def _params(n_axes=1):
    return pltpu.CompilerParams(
        dimension_semantics=("arbitrary",) * n_axes, vmem_limit_bytes=VMEM_LIMIT)


def _gmlp_body(x_ref, w_in_ref, b_in_ref, lng_ref, lnb_ref, wmix_ref, bmix_ref,
               w_out_ref, b_out_ref, g_ref, b_ref, o_ref, v_ref,
               v_scr, vn_scr, gated_scr, *, tm, dv, groups, alpha, v_rows,
               col_chunk):
    gd = dv // groups
    xb = x_ref[...].astype(BF16)
    n_cc = dv // col_chunk

    s1 = jnp.zeros((tm, 1), F32)
    for c in range(n_cc):
        cols = slice(dv + c * col_chunk, dv + (c + 1) * col_chunk)
        h = _gelu(_dot(xb, w_in_ref[:, cols]) + b_in_ref[:, cols])
        v_scr[:, c * col_chunk:(c + 1) * col_chunk] = h
        s1 = s1 + jnp.sum(h, axis=-1, keepdims=True)
    mu = s1 * (1.0 / dv)
    s2 = jnp.zeros((tm, 1), F32)
    for c in range(n_cc):
        d = v_scr[:, c * col_chunk:(c + 1) * col_chunk] - mu
        s2 = s2 + jnp.sum(d * d, axis=-1, keepdims=True)
    rstd = lax.rsqrt(s2 * (1.0 / dv) + LN_EPS)
    keep = tm if v_rows is None else v_rows
    for c in range(n_cc):
        cols = slice(c * col_chunk, (c + 1) * col_chunk)
        vn = (v_scr[:, cols] - mu) * rstd * lng_ref[:, cols] + lnb_ref[:, cols]
        v_ref[:, cols] = vn[tm - keep:, :]
        vn_scr[:, cols] = vn.astype(BF16)

    n_ch = tm // GMLP_CHUNK
    for g in range(groups):
        cols = slice(g * gd, (g + 1) * gd)
        u = _gelu(_dot(xb, w_in_ref[:, cols]) + b_in_ref[:, cols])
        vg = jnp.concatenate(
            [vn_scr[c * GMLP_CHUNK:(c + 1) * GMLP_CHUNK, cols] for c in range(n_ch)], axis=1)
        mixed = _dot(wmix_ref[g], vg) + bmix_ref[g]
        for c in range(n_ch):
            rows = slice(c * GMLP_CHUNK, (c + 1) * GMLP_CHUNK)
            gated_scr[rows, cols] = (u[rows, :] * mixed[:, c * gd:(c + 1) * gd]).astype(BF16)

    y = _dot(gated_scr[...], w_out_ref[...]) + b_out_ref[...]
    o_ref[...] = _ln(alpha * x_ref[...] + y, g_ref[...], b_ref[...])


def _gmlp_layer(x, w_in, b_in, lng, lnb, wmix, bmix, w_out, b_out, g, b, *,
                tm, alpha, seq_len, v_rows, col_chunk=768):
    n, d = x.shape
    dv = w_out.shape[0]
    groups = wmix.shape[0]
    assert n % tm == 0 and tm % GMLP_CHUNK == 0
    if v_rows is None:
        v_shape = (n, dv)
        v_spec = pl.BlockSpec((tm, dv), lambda i: (i, 0))
    else:
        assert seq_len % tm == 0 and v_rows <= tm
        tiles_per_seq = seq_len // tm
        v_shape = (n // seq_len * v_rows, dv)
        v_spec = pl.BlockSpec((v_rows, dv), lambda i: (i // tiles_per_seq, 0))
    body = functools.partial(
        _gmlp_body, tm=tm, dv=dv, groups=groups, alpha=alpha, v_rows=v_rows,
        col_chunk=col_chunk)
    return pl.pallas_call(
        body,
        grid=(n // tm,),
        in_specs=[
            pl.BlockSpec((tm, d), lambda i: (i, 0)),
            _full(w_in.shape), _full(b_in.shape), _full(lng.shape), _full(lnb.shape),
            _full(wmix.shape), _full(bmix.shape), _full(w_out.shape),
            _full(b_out.shape), _full(g.shape), _full(b.shape),
        ],
        out_specs=[pl.BlockSpec((tm, d), lambda i: (i, 0)), v_spec],
        out_shape=[jax.ShapeDtypeStruct((n, d), F32),
                   jax.ShapeDtypeStruct(v_shape, F32)],
        scratch_shapes=[pltpu.VMEM((tm, dv), F32), pltpu.VMEM((tm, dv), BF16),
                        pltpu.VMEM((tm, dv), BF16)],
        compiler_params=_params(),
        name="gmlp_layer",
    )(x, w_in, b_in, lng, lnb, wmix, bmix, w_out, b_out, g, b)


def _ffn_body(x_ref, wg_ref, wu_ref, wd_ref, g_ref, b_ref, o_ref, h_scr, *,
              alpha, f_chunk):
    xb = x_ref[...].astype(BF16)
    ff = wg_ref.shape[1]
    for c in range(ff // f_chunk):
        cols = slice(c * f_chunk, (c + 1) * f_chunk)
        gt = _dot(xb, wg_ref[:, cols])
        up = _dot(xb, wu_ref[:, cols])
        h_scr[:, cols] = (gt * _sigmoid(gt) * up).astype(BF16)
    y = _dot(h_scr[...], wd_ref[...])
    o_ref[...] = _ln(alpha * x_ref[...] + y, g_ref[...], b_ref[...])


def _ffn_layer(x, wg, wu, wd, g, b, *, tm, alpha):
    n, d = x.shape
    ff = wg.shape[1]
    assert n % tm == 0
    body = functools.partial(_ffn_body, alpha=alpha, f_chunk=512)
    return pl.pallas_call(
        body,
        grid=(n // tm,),
        in_specs=[pl.BlockSpec((tm, d), lambda i: (i, 0)), _full(wg.shape),
                  _full(wu.shape), _full(wd.shape), _full(g.shape), _full(b.shape)],
        out_specs=pl.BlockSpec((tm, d), lambda i: (i, 0)),
        out_shape=jax.ShapeDtypeStruct((n, d), F32),
        scratch_shapes=[pltpu.VMEM((tm, ff), BF16)],
        compiler_params=_params(),
        name="ffn_layer",
    )(x, wg, wu, wd, g, b)


HIST_ROWS = 32


def _conv_body(*refs, ts, n_sub, seq_tiles, use_cache, alpha, width, n_experts, t_chunk):
    if use_cache:
        (x_ref, cache_ref, w1_ref, b1_ref, wdw_ref, bdw_ref, cg_ref, cb_ref, w2_ref,
         b2_ref, g_ref, b_ref, wr_ref, br_ref, cnt_in_ref,
         o_ref, state_ref, ri_ref, rg_ref, cnt_out_ref,
         rows_scr, act_scr, cnt_scr) = refs
    else:
        (x_ref, w1_ref, b1_ref, wdw_ref, bdw_ref, cg_ref, cb_ref, w2_ref,
         b2_ref, g_ref, b_ref, wr_ref, br_ref, cnt_in_ref,
         o_ref, state_ref, ri_ref, rg_ref, cnt_out_ref,
         rows_scr, act_scr, cnt_scr) = refs
        cache_ref = None
    d = x_ref.shape[1]
    nb = d // LANES
    first = HIST_ROWS - (width - 1)
    i = pl.program_id(0)

    @pl.when(i == 0)
    def _():
        cnt_scr[...] = cnt_in_ref[...]

    sub_len = ts // n_sub
    seg = HIST_ROWS + sub_len
    if use_cache:
        for s in range(n_sub):
            for j in range(nb):
                rows_scr[j, s * seg:s * seg + HIST_ROWS, :] = (
                    cache_ref[s, :, j * LANES:(j + 1) * LANES])
    else:
        @pl.when(i % seq_tiles == 0)
        def _():
            rows_scr[:, 0:HIST_ROWS, :] = jnp.zeros((nb, HIST_ROWS, LANES), F32)

        @pl.when(i % seq_tiles != 0)
        def _():
            rows_scr[:, 0:HIST_ROWS, :] = rows_scr[:, ts:ts + HIST_ROWS, :]

    xb = x_ref[...].astype(BF16)
    cw = 2 * LANES
    for c in range(d // cw):
        a = _dot(xb, w1_ref[:, c * cw:(c + 1) * cw]) + b1_ref[:, c * cw:(c + 1) * cw]
        gt = (_dot(xb, w1_ref[:, d + c * cw:d + (c + 1) * cw])
              + b1_ref[:, d + c * cw:d + (c + 1) * cw])
        glu = a * _sigmoid(gt)
        for s in range(n_sub):
            dst = slice(s * seg + HIST_ROWS, (s + 1) * seg)
            src = slice(s * sub_len, (s + 1) * sub_len)
            rows_scr[2 * c, dst, :] = glu[src, :LANES]
            rows_scr[2 * c + 1, dst, :] = glu[src, LANES:]

    for s in range(n_sub):
        for j in range(nb):
            state_ref[s, :, j * LANES:(j + 1) * LANES] = (
                rows_scr[j, (s + 1) * seg - HIST_ROWS:(s + 1) * seg, :])

    def conv_chunk(src0, base):
        ys = []
        s1 = jnp.zeros((t_chunk, 1), F32)
        for j in range(nb):
            acc = jnp.zeros((t_chunk, LANES), F32)
            for k in range(width):
                tap = wdw_ref[j, k:k + 1, :]
                acc = acc + rows_scr[j, src0 + k:src0 + k + t_chunk, :] * tap
            acc = acc + bdw_ref[:, j * LANES:(j + 1) * LANES]
            ys.append(acc)
            s1 = s1 + jnp.sum(acc, axis=-1, keepdims=True)
        mu = s1 * (1.0 / d)
        s2 = jnp.zeros((t_chunk, 1), F32)
        for j in range(nb):
            dj = ys[j] - mu
            s2 = s2 + jnp.sum(dj * dj, axis=-1, keepdims=True)
        rstd = lax.rsqrt(s2 * (1.0 / d) + LN_EPS)
        for j in range(nb):
            cols = slice(j * LANES, (j + 1) * LANES)
            yn = (ys[j] - mu) * rstd * cg_ref[:, cols] + cb_ref[:, cols]
            act_scr[base:base + t_chunk, cols] = (yn * _sigmoid(yn)).astype(BF16)

    for s in range(n_sub):
        for tc in range(sub_len // t_chunk):
            conv_chunk(s * seg + first + tc * t_chunk, s * sub_len + tc * t_chunk)

    m = _dot(act_scr[...], w2_ref[...]) + b2_ref[...]
    x3 = _ln(alpha * x_ref[...] + m, g_ref[...], b_ref[...])
    _store_row_tiles(o_ref, x3)

    x3_hi = x3.astype(BF16)
    x3_lo = (x3 - x3_hi.astype(F32)).astype(BF16)
    hi_prod = _dot(x3_hi, wr_ref[...])
    logits = (hi_prod[:, :LANES] + (hi_prod[:, LANES:] + _dot(x3_lo, wr_ref[:, :LANES]))
              + br_ref[...])
    lane = lax.broadcasted_iota(jnp.int32, (ts, LANES), 1)
    neg = jnp.float32(-jnp.inf)
    logits = jnp.where(lane < n_experts, logits, neg)
    m1 = jnp.max(logits, axis=-1, keepdims=True)
    i1 = jnp.min(jnp.where(logits == m1, lane, LANES), axis=-1, keepdims=True)
    rest = jnp.where(lane == i1, neg, logits)
    m2 = jnp.max(rest, axis=-1, keepdims=True)
    i2 = jnp.min(jnp.where(rest == m2, lane, LANES), axis=-1, keepdims=True)
    e2 = jnp.exp(m2 - m1)
    den = 1.0 + e2
    g1 = 1.0 / den
    g2 = e2 / den
    sel1 = lane == i1
    sel2 = lane == i2
    onehot = jnp.where(sel1 | sel2, 1.0, 0.0)
    r_io = lax.broadcasted_iota(jnp.int32, (ts, ts), 0)
    c_io = lax.broadcasted_iota(jnp.int32, (ts, ts), 1)
    tri = jnp.where(c_io < r_io, 1.0, 0.0).astype(BF16)
    prefix = _dot(tri, onehot.astype(BF16)) + cnt_scr[...]
    p1 = jnp.sum(jnp.where(sel1, prefix, 0.0), axis=-1, keepdims=True).astype(jnp.int32)
    p2 = jnp.sum(jnp.where(sel2, prefix, 0.0), axis=-1, keepdims=True).astype(jnp.int32)
    cnt_new = cnt_scr[...] + jnp.sum(onehot, axis=0, keepdims=True)
    cnt_scr[...] = cnt_new
    cnt_out_ref[...] = cnt_new
    ri_ref[...] = jnp.where(lane == 0, i1, jnp.where(lane == 1, i2,
                  jnp.where(lane == 2, p1, jnp.where(lane == 3, p2, 0))))
    rg_ref[...] = jnp.where(lane == 0, g1, jnp.where(lane == 1, g2, 0.0))


def _conv_layer(x, cache, w1, b1, wdw, bdw, cg, cb, w2, b2, g, b, wr, br, cnt_in, *,
                ts, seq_len, alpha, width, n_experts):
    n, d = x.shape
    nb = d // LANES
    hist = HIST_ROWS
    use_cache = cache is not None
    if use_cache:
        assert ts % seq_len == 0
        n_sub, seq_tiles = ts // seq_len, 1
    else:
        assert seq_len % ts == 0
        n_sub, seq_tiles = 1, seq_len // ts
    sub_len = ts // n_sub
    assert n % ts == 0 and width - 1 <= HIST_ROWS <= sub_len and nb == SUBLANES
    t_chunk = min(sub_len, 128)
    body = functools.partial(
        _conv_body, ts=ts, n_sub=n_sub, seq_tiles=seq_tiles, use_cache=use_cache,
        alpha=alpha, width=width, n_experts=n_experts, t_chunk=t_chunk)
    weights = (w1, b1, wdw, bdw, cg, cb, w2, b2, g, b, wr, br, cnt_in)
    in_specs = [pl.BlockSpec((ts, d), lambda i: (i, 0))]
    args = [x]
    if use_cache:
        in_specs.append(pl.BlockSpec((n_sub, hist, d), lambda i: (i, 0, 0)))
        args.append(cache)
    in_specs += [_full(w.shape) for w in weights]
    args += list(weights)
    n_seq = n // seq_len
    return pl.pallas_call(
        body,
        grid=(n // ts,),
        in_specs=in_specs,
        out_specs=[
            pl.BlockSpec((ts * nb, LANES), lambda i: (i, 0)),
            pl.BlockSpec((n_sub, hist, d), lambda i: (i // seq_tiles, 0, 0)),
            pl.BlockSpec((ts, LANES), lambda i: (i, 0)),
            pl.BlockSpec((ts, LANES), lambda i: (i, 0)),
            _full((1, LANES)),
        ],
        out_shape=[
            jax.ShapeDtypeStruct((n * nb, LANES), F32),
            jax.ShapeDtypeStruct((n_seq, hist, d), F32),
            jax.ShapeDtypeStruct((n, LANES), jnp.int32),
            jax.ShapeDtypeStruct((n, LANES), F32),
            jax.ShapeDtypeStruct((1, LANES), F32),
        ],
        scratch_shapes=[pltpu.VMEM((nb, ts + n_sub * HIST_ROWS, LANES), F32),
                        pltpu.VMEM((ts, d), BF16), pltpu.VMEM((1, LANES), F32)],
        compiler_params=_params(),
        name="conv_layer",
    )(*args)


def _dispatch_body(dest_a_ref, dest_b_ref, fill_ref, nu_ref, xa_ref, xb_ref,
                   out_hbm, zeros_scr, sem, *, tb, steps_a, n_fill, n_tiles, tile_rows):
    i = pl.program_id(0)

    def fill(start):
        start = pl.multiple_of(start * SUBLANES, tile_rows * SUBLANES)
        return pltpu.make_async_copy(
            zeros_scr, out_hbm.at[pl.ds(start, tile_rows * SUBLANES)], sem)

    @pl.when(i == 0)
    def _():
        zeros_scr[...] = jnp.zeros(zeros_scr.shape, zeros_scr.dtype)
        for e in range(n_fill):
            fill(fill_ref[e]).start()
        for e in range(n_fill):
            fill(fill_ref[e]).wait()

        def fill_unused(j, carry):
            fill(j * tile_rows).start()
            fill(j * tile_rows).wait()
            return carry

        lax.fori_loop(nu_ref[0], n_tiles, fill_unused, 0)

    def scatter(x_ref, dest_ref):
        def issue(r, carry):
            for k in range(TOP_K):
                pltpu.make_async_copy(
                    _row_tile(x_ref, r), _row_tile(out_hbm, dest_ref[TOP_K * r + k]),
                    sem).start(priority=k % DMA_PRIORITIES)
            return carry

        lax.fori_loop(0, tb, issue, 0, unroll=8)
        for _ in range(TOP_K):
            pltpu.make_async_copy(
                x_ref, out_hbm.at[pl.ds(0, tb * SUBLANES)], sem).wait()

    @pl.when(i < steps_a)
    def _():
        scatter(xa_ref, dest_a_ref)

    @pl.when(i >= steps_a)
    def _():
        scatter(xb_ref, dest_b_ref)


def _dispatch(xa, dest_a, xb, dest_b, fill_starts, n_used, *, tb, n_tiles, tile_rows):
    na, nb = xa.shape[0] // SUBLANES, xb.shape[0] // SUBLANES
    assert na % tb == 0 and nb % tb == 0
    steps_a, steps_b = na // tb, nb // tb
    smem = pl.BlockSpec(memory_space=pltpu.SMEM)
    any_spec = pl.BlockSpec(memory_space=pl.ANY)
    body = functools.partial(
        _dispatch_body, tb=tb, steps_a=steps_a, n_fill=fill_starts.shape[0],
        n_tiles=n_tiles, tile_rows=tile_rows)
    return pl.pallas_call(
        body,
        grid=(steps_a + steps_b,),
        in_specs=[
            pl.BlockSpec((TOP_K * tb,), lambda i: (jnp.minimum(i, steps_a - 1),),
                         memory_space=pltpu.SMEM),
            pl.BlockSpec((TOP_K * tb,), lambda i: (jnp.maximum(i - steps_a, 0),),
                         memory_space=pltpu.SMEM),
            smem, smem,
            pl.BlockSpec((tb * SUBLANES, LANES), lambda i: (jnp.minimum(i, steps_a - 1), 0)),
            pl.BlockSpec((tb * SUBLANES, LANES), lambda i: (jnp.maximum(i - steps_a, 0), 0)),
        ],
        out_specs=any_spec,
        out_shape=jax.ShapeDtypeStruct((n_tiles * tile_rows * SUBLANES, LANES), xa.dtype),
        scratch_shapes=[pltpu.VMEM((tile_rows * SUBLANES, LANES), xa.dtype),
                        pltpu.SemaphoreType.DMA(())],
        compiler_params=_params(),
        name="moe_dispatch",
    )(dest_a, dest_b, fill_starts, n_used, xa, xb)


def _moe_body(tile_ref, te_ref, nu_ref, xs_ref, wg_ref, wu_ref, wd_ref, o_ref, h_scr, *,
              sub):
    del tile_ref, te_ref
    i = pl.program_id(0)
    tm = h_scr.shape[0]

    @pl.when(i < nu_ref[0])
    def _():
        xb = _load_row_tiles(xs_ref, tm).astype(BF16)
        for c in range(wg_ref.shape[2] // sub):
            cols = slice(c * sub, (c + 1) * sub)
            gt = _dot(xb, wg_ref[0, :, cols])
            up = _dot(xb, wu_ref[0, :, cols])
            h_scr[:, cols] = (gt * _sigmoid(gt) * up).astype(BF16)
        _store_row_tiles(o_ref, _dot(h_scr[...], wd_ref[0]))

    @pl.when(i >= nu_ref[0])
    def _():
        o_ref[...] = jnp.zeros(o_ref.shape, o_ref.dtype)


def _moe_grouped(xs, tile_idx, tile_expert, n_used, wg, wu, wd, *, tm, sub):
    r = xs.shape[0] // SUBLANES
    d, ff = wg.shape[1], wg.shape[2]
    assert r % tm == 0 and ff % sub == 0
    once = pl.Buffered(1)
    grid_spec = pltpu.PrefetchScalarGridSpec(
        num_scalar_prefetch=3,
        grid=(r // tm,),
        in_specs=[
            pl.BlockSpec((tm * SUBLANES, LANES), lambda i, ti, te, nu: (ti[i], 0)),
            pl.BlockSpec((1, d, ff), lambda i, ti, te, nu: (te[i], 0, 0), pipeline_mode=once),
            pl.BlockSpec((1, d, ff), lambda i, ti, te, nu: (te[i], 0, 0), pipeline_mode=once),
            pl.BlockSpec((1, ff, d), lambda i, ti, te, nu: (te[i], 0, 0), pipeline_mode=once),
        ],
        out_specs=pl.BlockSpec((tm * SUBLANES, LANES), lambda i, ti, te, nu: (i, 0)),
        scratch_shapes=[pltpu.VMEM((tm, ff), BF16)],
    )
    return pl.pallas_call(
        functools.partial(_moe_body, sub=sub),
        grid_spec=grid_spec,
        out_shape=jax.ShapeDtypeStruct(xs.shape, F32),
        compiler_params=_params(),
        name="moe_grouped",
    )(tile_idx, tile_expert, n_used, xs, wg, wu, wd)


def _combine_body(dest_ref, dest_next_ref, x_ref, rg_ref, ys_hbm, g_ref, b_ref, o_ref,
                  y_scr, sems, *, tb, alpha):
    i = pl.program_id(0)
    slot = i % 2

    def gather(d_ref, s):
        def issue(r, carry):
            for k in range(TOP_K):
                pltpu.make_async_copy(
                    _row_tile(ys_hbm, d_ref[TOP_K * r + k]), _row_tile(y_scr.at[s, k], r),
                    sems.at[s]).start(priority=k % DMA_PRIORITIES)
            return carry

        lax.fori_loop(0, tb, issue, 0, unroll=8)

    @pl.when(i == 0)
    def _():
        gather(dest_ref, slot)

    @pl.when(i + 1 < pl.num_programs(0))
    def _():
        gather(dest_next_ref, 1 - slot)

    for k in range(TOP_K):
        pltpu.make_async_copy(
            ys_hbm.at[pl.ds(0, tb * SUBLANES)], y_scr.at[slot, k], sems.at[slot]).wait()

    rg = rg_ref[...]
    moe = (rg[:, 0:1] * _load_row_tiles(y_scr.at[slot, 0], tb)
           + rg[:, 1:2] * _load_row_tiles(y_scr.at[slot, 1], tb))
    x = _load_row_tiles(x_ref, tb)
    o_ref[...] = _ln(alpha * x + moe, g_ref[...], b_ref[...])


def _combine(x, rg, dest, ys, g, b, *, tb, alpha):
    n, d = x.shape[0] // SUBLANES, SUBLANES * LANES
    assert n % tb == 0
    steps = n // tb
    body = functools.partial(_combine_body, tb=tb, alpha=alpha)
    return pl.pallas_call(
        body,
        grid=(steps,),
        in_specs=[
            pl.BlockSpec((TOP_K * tb,), lambda i: (i,), memory_space=pltpu.SMEM),
            pl.BlockSpec((TOP_K * tb,), lambda i: (jnp.minimum(i + 1, steps - 1),),
                         memory_space=pltpu.SMEM),
            pl.BlockSpec((tb * SUBLANES, LANES), lambda i: (i, 0)),
            pl.BlockSpec((tb, LANES), lambda i: (i, 0)),
            pl.BlockSpec(memory_space=pl.ANY),
            _full(g.shape), _full(b.shape),
        ],
        out_specs=pl.BlockSpec((tb, d), lambda i: (i, 0)),
        out_shape=jax.ShapeDtypeStruct((n, d), F32),
        scratch_shapes=[pltpu.VMEM((2, TOP_K, tb * SUBLANES, LANES), F32),
                        pltpu.SemaphoreType.DMA((2,))],
        compiler_params=_params(),
        name="moe_combine",
    )(dest, dest, x, rg, ys, g, b)


MOE_TILE = 1024
MOE_F_SUB = 256


def _row(v):
    return v.reshape(1, -1).astype(F32)


def kernel(x_prompt, x_sample, cache_conv, gm_w_in, gm_b_in, gm_lnv_g, gm_lnv_b, gm_w_s, gm_b_s, gm_w_out, gm_b_out, cv_w_pw1, cv_b_pw1, cv_w_dw, cv_b_dw, cv_ln_g, cv_ln_b, cv_w_pw2, cv_b_pw2, ff_w_gate, ff_w_up, ff_w_down, moe_w_router, moe_b_router, moe_w_gate, moe_w_up, moe_w_down, ln_g, ln_b):
    bp, sp, d = x_prompt.shape
    bs, ss, _ = x_sample.shape
    depth = ln_g.shape[0]
    assert depth == 2, "one gMLP layer followed by one conv/MoE layer"
    alpha = (2.0 * depth) ** 0.25
    n_experts = moe_w_router.shape[-1]
    width = cv_w_dw.shape[1]
    groups = gm_w_s.shape[1]
    np_, ns_ = bp * sp, bs * ss
    xp = x_prompt.reshape(np_, d)
    xs_ = x_sample.reshape(ns_, d)

    w_s = gm_w_s[0]
    pos = jnp.arange(GMLP_CHUNK) // CHUNK
    mask = pos[None, :] <= pos[:, None]
    wmix_p = jnp.where(mask[None], w_s, 0.0).astype(BF16)
    bmix_p = gm_b_s[0][:, :, None].astype(F32)
    assert GMLP_CHUNK % ss == 0 and ss <= CHUNK
    rep = GMLP_CHUNK // ss
    blk = w_s[:, :ss, :ss]
    eye = jnp.eye(rep, dtype=F32)
    wmix_s = jnp.einsum("ab,gij->gaibj", eye, blk).reshape(
        groups, GMLP_CHUNK, GMLP_CHUNK).astype(BF16)
    bmix_s = jnp.tile(gm_b_s[0][:, :ss], (1, rep))[:, :, None].astype(F32)

    gm_common = dict(alpha=alpha)
    gm_w = (gm_w_in[0].astype(BF16), _row(gm_b_in[0]), _row(gm_lnv_g[0]),
            _row(gm_lnv_b[0]))
    gm_tail = (gm_w_out[0].astype(BF16), _row(gm_b_out[0]), _row(ln_g[0, 0]),
               _row(ln_b[0, 0]))
    xp, vp = _gmlp_layer(xp, *gm_w, wmix_p, bmix_p, *gm_tail, tm=512, seq_len=sp,
                         v_rows=GMLP_CHUNK, **gm_common)
    xs_, vs = _gmlp_layer(xs_, *gm_w, wmix_s, bmix_s, *gm_tail, tm=256, seq_len=ss,
                          v_rows=None, **gm_common)
    gm_state_p = vp.reshape(1, bp, GMLP_CHUNK, -1)
    gm_state_s = vs.reshape(1, bs, ss, -1)

    ff_w = (ff_w_gate[0].astype(BF16), ff_w_up[0].astype(BF16),
            ff_w_down[0].astype(BF16), _row(ln_g[0, 1]), _row(ln_b[0, 1]))
    xp = _ffn_layer(xp, *ff_w, tm=512, alpha=alpha)
    xs_ = _ffn_layer(xs_, *ff_w, tm=512, alpha=alpha)

    nb = d // LANES
    wdw = jnp.transpose(cv_w_dw[0].reshape(width, nb, LANES), (1, 0, 2)).astype(F32)
    wr_f32 = jnp.zeros((d, LANES), F32).at[:, :n_experts].set(moe_w_router[0])
    wr_hi = wr_f32.astype(BF16)
    wr = jnp.concatenate([wr_hi, (wr_f32 - wr_hi.astype(F32)).astype(BF16)], axis=1)
    br = jnp.zeros((1, LANES), F32).at[0, :n_experts].set(moe_b_router[0])
    cv_w = (cv_w_pw1[0].astype(BF16), _row(cv_b_pw1[0]), wdw, _row(cv_b_dw[0]),
            _row(cv_ln_g[0]), _row(cv_ln_b[0]), cv_w_pw2[0].astype(BF16),
            _row(cv_b_pw2[0]), _row(ln_g[1, 0]), _row(ln_b[1, 0]), wr, br)
    cv_common = dict(alpha=alpha, width=width, n_experts=n_experts)
    cnt0 = jnp.zeros((1, LANES), F32)
    xp, st_p, ri_p, rg_p, cnt_p = _conv_layer(
        xp, None, *cv_w, cnt0, ts=512, seq_len=sp, **cv_common)
    hist = width - 1
    cache = jnp.pad(cache_conv[0].astype(F32), ((0, 0), (HIST_ROWS - hist, 0), (0, 0)))
    xs_, st_s, ri_s, rg_s, cnt_s = _conv_layer(
        xs_, cache, *cv_w, cnt_p, ts=512, seq_len=ss, **cv_common)
    conv_state_p = st_p[None, :, HIST_ROWS - hist:, :]
    conv_state_s = st_s[None, :, HIST_ROWS - hist:, :]

    tm = MOE_TILE
    n_assign = (np_ + ns_) * TOP_K
    n_tiles = (n_assign + n_experts * (tm - 1)) // tm
    counts = cnt_s[0, :n_experts].astype(jnp.int32)
    tiles_e = (counts + tm - 1) // tm
    tile_end = jnp.cumsum(tiles_e)
    offs = (tile_end - tiles_e) * tm
    n_used = tile_end[-1:]
    tile_idx = jnp.minimum(jnp.arange(n_tiles, dtype=jnp.int32), n_used[0] - 1)
    tile_expert = jnp.minimum(
        jnp.sum(tile_idx[:, None] >= tile_end[None, :], axis=1), n_experts - 1
    ).astype(jnp.int32)
    fill_starts = (jnp.maximum(tile_end, 1) - 1).astype(jnp.int32) * tm

    def dest_of(ri):
        e_idx = ri[:, 0:TOP_K]
        sel = e_idx[:, :, None] == jnp.arange(n_experts, dtype=jnp.int32)
        start = jnp.sum(jnp.where(sel, offs.astype(jnp.int32), 0), axis=-1)
        return (start + ri[:, TOP_K:2 * TOP_K]).reshape(-1).astype(jnp.int32)

    dest_p, dest_s = dest_of(ri_p), dest_of(ri_s)
    n_used = n_used.astype(jnp.int32)
    xsort = _dispatch(xp, dest_p, xs_, dest_s, fill_starts, n_used, tb=1024,
                      n_tiles=n_tiles, tile_rows=tm)
    ysort = _moe_grouped(
        xsort, tile_idx, tile_expert, n_used,
        moe_w_gate[0].astype(BF16), moe_w_up[0].astype(BF16),
        moe_w_down[0].astype(BF16), tm=tm, sub=MOE_F_SUB)
    fin = (_row(ln_g[1, 1]), _row(ln_b[1, 1]))
    yp = _combine(xp, rg_p, dest_p, ysort, *fin, tb=512, alpha=alpha)
    ys_ = _combine(xs_, rg_s, dest_s, ysort, *fin, tb=512, alpha=alpha)

    return (yp.reshape(bp, sp, d), ys_.reshape(bs, ss, d), gm_state_p, gm_state_s,
            conv_state_p, conv_state_s)
```

```python
import functools

import jax
import jax.numpy as jnp
from jax import lax
from jax.experimental import pallas as pl
from jax.experimental.pallas import tpu as pltpu

F32 = jnp.float32
BF16 = jnp.bfloat16

LANES = 128
SUBLANES = 8
LN_EPS = 1e-5
CHUNK = 64
GMLP_CHUNK = 128
TOP_K = 2
DMA_PRIORITIES = 2
VMEM_LIMIT = 56 * 1024 * 1024


def _ln(z, g, b):
    mu = jnp.mean(z, axis=-1, keepdims=True)
    zc = z - mu
    var = jnp.mean(zc * zc, axis=-1, keepdims=True)
    return zc * lax.rsqrt(var + LN_EPS) * g + b


def _gelu(x):
    return 0.5 * x * (1.0 + lax.erf(x * (0.5 ** 0.5)))


def _sigmoid(x):
    return 1.0 / (1.0 + jnp.exp(-x))


def _dot(a, b):
    return jnp.dot(a, b, preferred_element_type=F32)


def _load_row_tiles(ref, rows):
    return jnp.concatenate(
        [ref[pl.ds(j, rows, stride=SUBLANES), :] for j in range(SUBLANES)], axis=1)


def _row_tile(ref, r):
    return ref.at[pl.ds(pl.multiple_of(r * SUBLANES, SUBLANES), SUBLANES)]


def _store_row_tiles(ref, value):
    rows = value.shape[0]
    for j in range(SUBLANES):
        ref[pl.ds(j, rows, stride=SUBLANES), :] = value[:, j * LANES:(j + 1) * LANES]


def _full(shape):
    zeros = (0,) * len(shape)
    return pl.BlockSpec(shape, lambda *_: zeros)


def _params(n_axes=1):
    return pltpu.CompilerParams(
        dimension_semantics=("arbitrary",) * n_axes, vmem_limit_bytes=VMEM_LIMIT)


def _gmlp_body(x_ref, w_in_ref, b_in_ref, lng_ref, lnb_ref, wmix_ref, bmix_ref,
               w_out_ref, b_out_ref, g_ref, b_ref, o_ref, v_ref,
               v_scr, vn_scr, gated_scr, *, tm, dv, groups, alpha, v_rows,
               col_chunk):
    gd = dv // groups
    xb = x_ref[...].astype(BF16)
    n_cc = dv // col_chunk

    s1 = jnp.zeros((tm, 1), F32)
    s2 = jnp.zeros((tm, 1), F32)
    for c in range(n_cc):
        cols = slice(dv + c * col_chunk, dv + (c + 1) * col_chunk)
        h = _gelu(_dot(xb, w_in_ref[:, cols]) + b_in_ref[:, cols])
        v_scr[:, c * col_chunk:(c + 1) * col_chunk] = h
        s1 = s1 + jnp.sum(h, axis=-1, keepdims=True)
        s2 = s2 + jnp.sum(h * h, axis=-1, keepdims=True)
    mu = s1 * (1.0 / dv)
    rstd = lax.rsqrt(s2 * (1.0 / dv) - mu * mu + LN_EPS)
    keep = tm if v_rows is None else v_rows
    for c in range(n_cc):
        cols = slice(c * col_chunk, (c + 1) * col_chunk)
        vn = (v_scr[:, cols] - mu) * rstd * lng_ref[:, cols] + lnb_ref[:, cols]
        v_ref[:, cols] = vn[tm - keep:, :]
        vn_scr[:, cols] = vn.astype(BF16)

    n_ch = tm // GMLP_CHUNK
    for g in range(groups):
        cols = slice(g * gd, (g + 1) * gd)
        u = _gelu(_dot(xb, w_in_ref[:, cols]) + b_in_ref[:, cols])
        vg = jnp.concatenate(
            [vn_scr[c * GMLP_CHUNK:(c + 1) * GMLP_CHUNK, cols] for c in range(n_ch)], axis=1)
        mixed = _dot(wmix_ref[g], vg) + bmix_ref[g]
        for c in range(n_ch):
            rows = slice(c * GMLP_CHUNK, (c + 1) * GMLP_CHUNK)
            gated_scr[rows, cols] = (u[rows, :] * mixed[:, c * gd:(c + 1) * gd]).astype(BF16)

    y = _dot(gated_scr[...], w_out_ref[...]) + b_out_ref[...]
    o_ref[...] = _ln(alpha * x_ref[...] + y, g_ref[...], b_ref[...])


def _gmlp_layer(x, w_in, b_in, lng, lnb, wmix, bmix, w_out, b_out, g, b, *,
                tm, alpha, seq_len, v_rows, col_chunk=768):
    n, d = x.shape
    dv = w_out.shape[0]
    groups = wmix.shape[0]
    assert n % tm == 0 and tm % GMLP_CHUNK == 0
    if v_rows is None:
        v_shape = (n, dv)
        v_spec = pl.BlockSpec((tm, dv), lambda i: (i, 0))
    else:
        assert seq_len % tm == 0 and v_rows <= tm
        tiles_per_seq = seq_len // tm
        v_shape = (n // seq_len * v_rows, dv)
        v_spec = pl.BlockSpec((v_rows, dv), lambda i: (i // tiles_per_seq, 0))
    body = functools.partial(
        _gmlp_body, tm=tm, dv=dv, groups=groups, alpha=alpha, v_rows=v_rows,
        col_chunk=col_chunk)
    return pl.pallas_call(
        body,
        grid=(n // tm,),
        in_specs=[
            pl.BlockSpec((tm, d), lambda i: (i, 0)),
            _full(w_in.shape), _full(b_in.shape), _full(lng.shape), _full(lnb.shape),
            _full(wmix.shape), _full(bmix.shape), _full(w_out.shape),
            _full(b_out.shape), _full(g.shape), _full(b.shape),
        ],
        out_specs=[pl.BlockSpec((tm, d), lambda i: (i, 0)), v_spec],
        out_shape=[jax.ShapeDtypeStruct((n, d), F32),
                   jax.ShapeDtypeStruct(v_shape, F32)],
        scratch_shapes=[pltpu.VMEM((tm, dv), F32), pltpu.VMEM((tm, dv), BF16),
                        pltpu.VMEM((tm, dv), BF16)],
        compiler_params=_params(),
        name="gmlp_layer",
    )(x, w_in, b_in, lng, lnb, wmix, bmix, w_out, b_out, g, b)


def _ffn_body(x_ref, wg_ref, wu_ref, wd_ref, g_ref, b_ref, o_ref, h_scr, *,
              alpha, f_chunk):
    xb = x_ref[...].astype(BF16)
    ff = wg_ref.shape[1]
    for c in range(ff // f_chunk):
        cols = slice(c * f_chunk, (c + 1) * f_chunk)
        gt = _dot(xb, wg_ref[:, cols])
        up = _dot(xb, wu_ref[:, cols])
        h_scr[:, cols] = (gt * _sigmoid(gt) * up).astype(BF16)
    y = _dot(h_scr[...], wd_ref[...])
    o_ref[...] = _ln(alpha * x_ref[...] + y, g_ref[...], b_ref[...])


def _ffn_layer(x, wg, wu, wd, g, b, *, tm, alpha):
    n, d = x.shape
    ff = wg.shape[1]
    assert n % tm == 0
    body = functools.partial(_ffn_body, alpha=alpha, f_chunk=512)
    return pl.pallas_call(
        body,
        grid=(n // tm,),
        in_specs=[pl.BlockSpec((tm, d), lambda i: (i, 0)), _full(wg.shape),
                  _full(wu.shape), _full(wd.shape), _full(g.shape), _full(b.shape)],
        out_specs=pl.BlockSpec((tm, d), lambda i: (i, 0)),
        out_shape=jax.ShapeDtypeStruct((n, d), F32),
        scratch_shapes=[pltpu.VMEM((tm, ff), BF16)],
        compiler_params=_params(),
        name="ffn_layer",
    )(x, wg, wu, wd, g, b)


HIST_ROWS = 32


def _conv_body(*refs, ts, n_sub, seq_tiles, use_cache, alpha, width, n_experts, t_chunk):
    if use_cache:
        (x_ref, cache_ref, w1_ref, b1_ref, wdw_ref, bdw_ref, cg_ref, cb_ref, w2_ref,
         b2_ref, g_ref, b_ref, wr_ref, br_ref, cnt_in_ref,
         o_ref, state_ref, ri_ref, rg_ref, cnt_out_ref,
         rows_scr, act_scr, cnt_scr) = refs
    else:
        (x_ref, w1_ref, b1_ref, wdw_ref, bdw_ref, cg_ref, cb_ref, w2_ref,
         b2_ref, g_ref, b_ref, wr_ref, br_ref, cnt_in_ref,
         o_ref, state_ref, ri_ref, rg_ref, cnt_out_ref,
         rows_scr, act_scr, cnt_scr) = refs
        cache_ref = None
    d = x_ref.shape[1]
    nb = d // LANES
    first = HIST_ROWS - (width - 1)
    i = pl.program_id(0)

    @pl.when(i == 0)
    def _():
        cnt_scr[...] = cnt_in_ref[...]

    sub_len = ts // n_sub
    seg = HIST_ROWS + sub_len
    if use_cache:
        for s in range(n_sub):
            for j in range(nb):
                rows_scr[j, s * seg:s * seg + HIST_ROWS, :] = (
                    cache_ref[s, :, j * LANES:(j + 1) * LANES])
    else:
        @pl.when(i % seq_tiles == 0)
        def _():
            rows_scr[:, 0:HIST_ROWS, :] = jnp.zeros((nb, HIST_ROWS, LANES), F32)

        @pl.when(i % seq_tiles != 0)
        def _():
            rows_scr[:, 0:HIST_ROWS, :] = rows_scr[:, ts:ts + HIST_ROWS, :]

    xb = x_ref[...].astype(BF16)
    cw = 2 * LANES
    for c in range(d // cw):
        a = _dot(xb, w1_ref[:, c * cw:(c + 1) * cw]) + b1_ref[:, c * cw:(c + 1) * cw]
        gt = (_dot(xb, w1_ref[:, d + c * cw:d + (c + 1) * cw])
              + b1_ref[:, d + c * cw:d + (c + 1) * cw])
        glu = a * _sigmoid(gt)
        for s in range(n_sub):
            dst = slice(s * seg + HIST_ROWS, (s + 1) * seg)
            src = slice(s * sub_len, (s + 1) * sub_len)
            rows_scr[2 * c, dst, :] = glu[src, :LANES]
            rows_scr[2 * c + 1, dst, :] = glu[src, LANES:]

    for s in range(n_sub):
        for j in range(nb):
            state_ref[s, :, j * LANES:(j + 1) * LANES] = (
                rows_scr[j, (s + 1) * seg - HIST_ROWS:(s + 1) * seg, :])

    def conv_chunk(src0, base):
        ys = []
        s1 = jnp.zeros((t_chunk, 1), F32)
        for j in range(nb):
            acc = jnp.zeros((t_chunk, LANES), F32)
            for k in range(width):
                tap = wdw_ref[j, k:k + 1, :]
                acc = acc + rows_scr[j, src0 + k:src0 + k + t_chunk, :] * tap
            acc = acc + bdw_ref[:, j * LANES:(j + 1) * LANES]
            ys.append(acc)
            s1 = s1 + jnp.sum(acc, axis=-1, keepdims=True)
        mu = s1 * (1.0 / d)
        s2 = jnp.zeros((t_chunk, 1), F32)
        for j in range(nb):
            dj = ys[j] - mu
            s2 = s2 + jnp.sum(dj * dj, axis=-1, keepdims=True)
        rstd = lax.rsqrt(s2 * (1.0 / d) + LN_EPS)
        for j in range(nb):
            cols = slice(j * LANES, (j + 1) * LANES)
            yn = (ys[j] - mu) * rstd * cg_ref[:, cols] + cb_ref[:, cols]
            act_scr[base:base + t_chunk, cols] = (yn * _sigmoid(yn)).astype(BF16)

    for s in range(n_sub):
        for tc in range(sub_len // t_chunk):
            conv_chunk(s * seg + first + tc * t_chunk, s * sub_len + tc * t_chunk)

    m = _dot(act_scr[...], w2_ref[...]) + b2_ref[...]
    x3 = _ln(alpha * x_ref[...] + m, g_ref[...], b_ref[...])
    _store_row_tiles(o_ref, x3)

    x3_hi = x3.astype(BF16)
    x3_lo = (x3 - x3_hi.astype(F32)).astype(BF16)
    hi_prod = _dot(x3_hi, wr_ref[...])
    logits = (hi_prod[:, :LANES] + (hi_prod[:, LANES:] + _dot(x3_lo, wr_ref[:, :LANES]))
              + br_ref[...])
    lane = lax.broadcasted_iota(jnp.int32, (ts, LANES), 1)
    neg = jnp.float32(-jnp.inf)
    logits = jnp.where(lane < n_experts, logits, neg)
    m1 = jnp.max(logits, axis=-1, keepdims=True)
    i1 = jnp.min(jnp.where(logits == m1, lane, LANES), axis=-1, keepdims=True)
    rest = jnp.where(lane == i1, neg, logits)
    m2 = jnp.max(rest, axis=-1, keepdims=True)
    i2 = jnp.min(jnp.where(rest == m2, lane, LANES), axis=-1, keepdims=True)
    e2 = jnp.exp(m2 - m1)
    den = 1.0 + e2
    g1 = 1.0 / den
    g2 = e2 / den
    sel1 = lane == i1
    sel2 = lane == i2
    onehot = jnp.where(sel1 | sel2, 1.0, 0.0)
    r_io = lax.broadcasted_iota(jnp.int32, (ts, ts), 0)
    c_io = lax.broadcasted_iota(jnp.int32, (ts, ts), 1)
    tri = jnp.where(c_io < r_io, 1.0, 0.0).astype(BF16)
    prefix = _dot(tri, onehot.astype(BF16)) + cnt_scr[...]
    p1 = jnp.sum(jnp.where(sel1, prefix, 0.0), axis=-1, keepdims=True).astype(jnp.int32)
    p2 = jnp.sum(jnp.where(sel2, prefix, 0.0), axis=-1, keepdims=True).astype(jnp.int32)
    cnt_new = cnt_scr[...] + jnp.sum(onehot, axis=0, keepdims=True)
    cnt_scr[...] = cnt_new
    cnt_out_ref[...] = cnt_new
    ri_ref[...] = jnp.where(lane == 0, i1, jnp.where(lane == 1, i2,
                  jnp.where(lane == 2, p1, jnp.where(lane == 3, p2, 0))))
    rg_ref[...] = jnp.where(lane == 0, g1, jnp.where(lane == 1, g2, 0.0))


def _conv_layer(x, cache, w1, b1, wdw, bdw, cg, cb, w2, b2, g, b, wr, br, cnt_in, *,
                ts, seq_len, alpha, width, n_experts):
    n, d = x.shape
    nb = d // LANES
    hist = HIST_ROWS
    use_cache = cache is not None
    if use_cache:
        assert ts % seq_len == 0
        n_sub, seq_tiles = ts // seq_len, 1
    else:
        assert seq_len % ts == 0
        n_sub, seq_tiles = 1, seq_len // ts
    sub_len = ts // n_sub
    assert n % ts == 0 and width - 1 <= HIST_ROWS <= sub_len and nb == SUBLANES
    t_chunk = min(sub_len, 128)
    body = functools.partial(
        _conv_body, ts=ts, n_sub=n_sub, seq_tiles=seq_tiles, use_cache=use_cache,
        alpha=alpha, width=width, n_experts=n_experts, t_chunk=t_chunk)
    weights = (w1, b1, wdw, bdw, cg, cb, w2, b2, g, b, wr, br, cnt_in)
    in_specs = [pl.BlockSpec((ts, d), lambda i: (i, 0))]
    args = [x]
    if use_cache:
        in_specs.append(pl.BlockSpec((n_sub, hist, d), lambda i: (i, 0, 0)))
        args.append(cache)
    in_specs += [_full(w.shape) for w in weights]
    args += list(weights)
    n_seq = n // seq_len
    return pl.pallas_call(
        body,
        grid=(n // ts,),
        in_specs=in_specs,
        out_specs=[
            pl.BlockSpec((ts * nb, LANES), lambda i: (i, 0)),
            pl.BlockSpec((n_sub, hist, d), lambda i: (i // seq_tiles, 0, 0)),
            pl.BlockSpec((ts, LANES), lambda i: (i, 0)),
            pl.BlockSpec((ts, LANES), lambda i: (i, 0)),
            _full((1, LANES)),
        ],
        out_shape=[
            jax.ShapeDtypeStruct((n * nb, LANES), F32),
            jax.ShapeDtypeStruct((n_seq, hist, d), F32),
            jax.ShapeDtypeStruct((n, LANES), jnp.int32),
            jax.ShapeDtypeStruct((n, LANES), F32),
            jax.ShapeDtypeStruct((1, LANES), F32),
        ],
        scratch_shapes=[pltpu.VMEM((nb, ts + n_sub * HIST_ROWS, LANES), F32),
                        pltpu.VMEM((ts, d), BF16), pltpu.VMEM((1, LANES), F32)],
        compiler_params=_params(),
        name="conv_layer",
    )(*args)


def _dispatch_body(dest_a_ref, dest_b_ref, fill_ref, nu_ref, xa_ref, xb_ref,
                   out_hbm, zeros_scr, sem, *, tb, steps_a, n_fill, n_tiles, tile_rows):
    i = pl.program_id(0)

    def fill(start):
        start = pl.multiple_of(start * SUBLANES, tile_rows * SUBLANES)
        return pltpu.make_async_copy(
            zeros_scr, out_hbm.at[pl.ds(start, tile_rows * SUBLANES)], sem)

    @pl.when(i == 0)
    def _():
        zeros_scr[...] = jnp.zeros(zeros_scr.shape, zeros_scr.dtype)
        for e in range(n_fill):
            fill(fill_ref[e]).start()
        for e in range(n_fill):
            fill(fill_ref[e]).wait()

        def fill_unused(j, carry):
            fill(j * tile_rows).start()
            fill(j * tile_rows).wait()
            return carry

        lax.fori_loop(nu_ref[0], n_tiles, fill_unused, 0)

    def scatter(x_ref, dest_ref):
        def issue(r, carry):
            for k in range(TOP_K):
                pltpu.make_async_copy(
                    _row_tile(x_ref, r), _row_tile(out_hbm, dest_ref[TOP_K * r + k]),
                    sem).start(priority=k % DMA_PRIORITIES)
            return carry

        lax.fori_loop(0, tb, issue, 0, unroll=8)
        for _ in range(TOP_K):
            pltpu.make_async_copy(
                x_ref, out_hbm.at[pl.ds(0, tb * SUBLANES)], sem).wait()

    @pl.when(i < steps_a)
    def _():
        scatter(xa_ref, dest_a_ref)

    @pl.when(i >= steps_a)
    def _():
        scatter(xb_ref, dest_b_ref)


def _dispatch(xa, dest_a, xb, dest_b, fill_starts, n_used, *, tb, n_tiles, tile_rows):
    na, nb = xa.shape[0] // SUBLANES, xb.shape[0] // SUBLANES
    assert na % tb == 0 and nb % tb == 0
    steps_a, steps_b = na // tb, nb // tb
    smem = pl.BlockSpec(memory_space=pltpu.SMEM)
    any_spec = pl.BlockSpec(memory_space=pl.ANY)
    body = functools.partial(
        _dispatch_body, tb=tb, steps_a=steps_a, n_fill=fill_starts.shape[0],
        n_tiles=n_tiles, tile_rows=tile_rows)
    return pl.pallas_call(
        body,
        grid=(steps_a + steps_b,),
        in_specs=[
            pl.BlockSpec((TOP_K * tb,), lambda i: (jnp.minimum(i, steps_a - 1),),
                         memory_space=pltpu.SMEM),
            pl.BlockSpec((TOP_K * tb,), lambda i: (jnp.maximum(i - steps_a, 0),),
                         memory_space=pltpu.SMEM),
            smem, smem,
            pl.BlockSpec((tb * SUBLANES, LANES), lambda i: (jnp.minimum(i, steps_a - 1), 0)),
            pl.BlockSpec((tb * SUBLANES, LANES), lambda i: (jnp.maximum(i - steps_a, 0), 0)),
        ],
        out_specs=any_spec,
        out_shape=jax.ShapeDtypeStruct((n_tiles * tile_rows * SUBLANES, LANES), xa.dtype),
        scratch_shapes=[pltpu.VMEM((tile_rows * SUBLANES, LANES), xa.dtype),
                        pltpu.SemaphoreType.DMA(())],
        compiler_params=_params(),
        name="moe_dispatch",
    )(dest_a, dest_b, fill_starts, n_used, xa, xb)


def _moe_body(tile_ref, te_ref, nu_ref, xs_ref, wg_ref, wu_ref, wd_ref, o_ref, h_scr, *,
              sub):
    del tile_ref, te_ref
    i = pl.program_id(0)
    tm = h_scr.shape[0]

    @pl.when(i < nu_ref[0])
    def _():
        xb = _load_row_tiles(xs_ref, tm).astype(BF16)
        for c in range(wg_ref.shape[2] // sub):
            cols = slice(c * sub, (c + 1) * sub)
            gt = _dot(xb, wg_ref[0, :, cols])
            up = _dot(xb, wu_ref[0, :, cols])
            h_scr[:, cols] = (gt * _sigmoid(gt) * up).astype(BF16)
        _store_row_tiles(o_ref, _dot(h_scr[...], wd_ref[0]))

    @pl.when(i >= nu_ref[0])
    def _():
        o_ref[...] = jnp.zeros(o_ref.shape, o_ref.dtype)


def _moe_grouped(xs, tile_idx, tile_expert, n_used, wg, wu, wd, *, tm, sub):
    r = xs.shape[0] // SUBLANES
    d, ff = wg.shape[1], wg.shape[2]
    assert r % tm == 0 and ff % sub == 0
    once = pl.Buffered(1)
    grid_spec = pltpu.PrefetchScalarGridSpec(
        num_scalar_prefetch=3,
        grid=(r // tm,),
        in_specs=[
            pl.BlockSpec((tm * SUBLANES, LANES), lambda i, ti, te, nu: (ti[i], 0)),
            pl.BlockSpec((1, d, ff), lambda i, ti, te, nu: (te[i], 0, 0), pipeline_mode=once),
            pl.BlockSpec((1, d, ff), lambda i, ti, te, nu: (te[i], 0, 0), pipeline_mode=once),
            pl.BlockSpec((1, ff, d), lambda i, ti, te, nu: (te[i], 0, 0), pipeline_mode=once),
        ],
        out_specs=pl.BlockSpec((tm * SUBLANES, LANES), lambda i, ti, te, nu: (i, 0)),
        scratch_shapes=[pltpu.VMEM((tm, ff), BF16)],
    )
    return pl.pallas_call(
        functools.partial(_moe_body, sub=sub),
        grid_spec=grid_spec,
        out_shape=jax.ShapeDtypeStruct(xs.shape, F32),
        compiler_params=_params(),
        name="moe_grouped",
    )(tile_idx, tile_expert, n_used, xs, wg, wu, wd)


def _combine_body(dest_ref, dest_next_ref, x_ref, rg_ref, ys_hbm, g_ref, b_ref, o_ref,
                  y_scr, sems, *, tb, alpha):
    i = pl.program_id(0)
    slot = i % 2

    def gather(d_ref, s):
        def issue(r, carry):
            for k in range(TOP_K):
                pltpu.make_async_copy(
                    _row_tile(ys_hbm, d_ref[TOP_K * r + k]), _row_tile(y_scr.at[s, k], r),
                    sems.at[s]).start(priority=k % DMA_PRIORITIES)
            return carry

        lax.fori_loop(0, tb, issue, 0, unroll=8)

    @pl.when(i == 0)
    def _():
        gather(dest_ref, slot)

    @pl.when(i + 1 < pl.num_programs(0))
    def _():
        gather(dest_next_ref, 1 - slot)

    for k in range(TOP_K):
        pltpu.make_async_copy(
            ys_hbm.at[pl.ds(0, tb * SUBLANES)], y_scr.at[slot, k], sems.at[slot]).wait()

    rg = rg_ref[...]
    moe = (rg[:, 0:1] * _load_row_tiles(y_scr.at[slot, 0], tb)
           + rg[:, 1:2] * _load_row_tiles(y_scr.at[slot, 1], tb))
    x = _load_row_tiles(x_ref, tb)
    o_ref[...] = _ln(alpha * x + moe, g_ref[...], b_ref[...])


def _combine(x, rg, dest, ys, g, b, *, tb, alpha):
    n, d = x.shape[0] // SUBLANES, SUBLANES * LANES
    assert n % tb == 0
    steps = n // tb
    body = functools.partial(_combine_body, tb=tb, alpha=alpha)
    return pl.pallas_call(
        body,
        grid=(steps,),
        in_specs=[
            pl.BlockSpec((TOP_K * tb,), lambda i: (i,), memory_space=pltpu.SMEM),
            pl.BlockSpec((TOP_K * tb,), lambda i: (jnp.minimum(i + 1, steps - 1),),
                         memory_space=pltpu.SMEM),
            pl.BlockSpec((tb * SUBLANES, LANES), lambda i: (i, 0)),
            pl.BlockSpec((tb, LANES), lambda i: (i, 0)),
            pl.BlockSpec(memory_space=pl.ANY),
            _full(g.shape), _full(b.shape),
        ],
        out_specs=pl.BlockSpec((tb, d), lambda i: (i, 0)),
        out_shape=jax.ShapeDtypeStruct((n, d), F32),
        scratch_shapes=[pltpu.VMEM((2, TOP_K, tb * SUBLANES, LANES), F32),
                        pltpu.SemaphoreType.DMA((2,))],
        compiler_params=_params(),
        name="moe_combine",
    )(dest, dest, x, rg, ys, g, b)


MOE_TILE = 1024
MOE_F_SUB = 256


def _row(v):
    return v.reshape(1, -1).astype(F32)


def kernel(x_prompt, x_sample, cache_conv, gm_w_in, gm_b_in, gm_lnv_g, gm_lnv_b, gm_w_s, gm_b_s, gm_w_out, gm_b_out, cv_w_pw1, cv_b_pw1, cv_w_dw, cv_b_dw, cv_ln_g, cv_ln_b, cv_w_pw2, cv_b_pw2, ff_w_gate, ff_w_up, ff_w_down, moe_w_router, moe_b_router, moe_w_gate, moe_w_up, moe_w_down, ln_g, ln_b):
    bp, sp, d = x_prompt.shape
    bs, ss, _ = x_sample.shape
    depth = ln_g.shape[0]
    assert depth == 2, "one gMLP layer followed by one conv/MoE layer"
    alpha = (2.0 * depth) ** 0.25
    n_experts = moe_w_router.shape[-1]
    width = cv_w_dw.shape[1]
    groups = gm_w_s.shape[1]
    np_, ns_ = bp * sp, bs * ss
    xp = x_prompt.reshape(np_, d)
    xs_ = x_sample.reshape(ns_, d)

    w_s = gm_w_s[0]
    pos = jnp.arange(GMLP_CHUNK) // CHUNK
    mask = pos[None, :] <= pos[:, None]
    wmix_p = jnp.where(mask[None], w_s, 0.0).astype(BF16)
    bmix_p = gm_b_s[0][:, :, None].astype(F32)
    assert GMLP_CHUNK % ss == 0 and ss <= CHUNK
    rep = GMLP_CHUNK // ss
    blk = w_s[:, :ss, :ss]
    eye = jnp.eye(rep, dtype=F32)
    wmix_s = jnp.einsum("ab,gij->gaibj", eye, blk).reshape(
        groups, GMLP_CHUNK, GMLP_CHUNK).astype(BF16)
    bmix_s = jnp.tile(gm_b_s[0][:, :ss], (1, rep))[:, :, None].astype(F32)

    gm_common = dict(alpha=alpha)
    gm_w = (gm_w_in[0].astype(BF16), _row(gm_b_in[0]), _row(gm_lnv_g[0]),
            _row(gm_lnv_b[0]))
    gm_tail = (gm_w_out[0].astype(BF16), _row(gm_b_out[0]), _row(ln_g[0, 0]),
               _row(ln_b[0, 0]))
    xp, vp = _gmlp_layer(xp, *gm_w, wmix_p, bmix_p, *gm_tail, tm=512, seq_len=sp,
                         v_rows=GMLP_CHUNK, **gm_common)
    xs_, vs = _gmlp_layer(xs_, *gm_w, wmix_s, bmix_s, *gm_tail, tm=256, seq_len=ss,
                          v_rows=None, **gm_common)
    gm_state_p = vp.reshape(1, bp, GMLP_CHUNK, -1)
    gm_state_s = vs.reshape(1, bs, ss, -1)

    ff_w = (ff_w_gate[0].astype(BF16), ff_w_up[0].astype(BF16),
            ff_w_down[0].astype(BF16), _row(ln_g[0, 1]), _row(ln_b[0, 1]))
    xp = _ffn_layer(xp, *ff_w, tm=512, alpha=alpha)
    xs_ = _ffn_layer(xs_, *ff_w, tm=512, alpha=alpha)

    nb = d // LANES
    wdw = jnp.transpose(cv_w_dw[0].reshape(width, nb, LANES), (1, 0, 2)).astype(F32)
    wr_f32 = jnp.zeros((d, LANES), F32).at[:, :n_experts].set(moe_w_router[0])
    wr_hi = wr_f32.astype(BF16)
    wr = jnp.concatenate([wr_hi, (wr_f32 - wr_hi.astype(F32)).astype(BF16)], axis=1)
    br = jnp.zeros((1, LANES), F32).at[0, :n_experts].set(moe_b_router[0])
    cv_w = (cv_w_pw1[0].astype(BF16), _row(cv_b_pw1[0]), wdw, _row(cv_b_dw[0]),
            _row(cv_ln_g[0]), _row(cv_ln_b[0]), cv_w_pw2[0].astype(BF16),
            _row(cv_b_pw2[0]), _row(ln_g[1, 0]), _row(ln_b[1, 0]), wr, br)
    cv_common = dict(alpha=alpha, width=width, n_experts=n_experts)
    cnt0 = jnp.zeros((1, LANES), F32)
    xp, st_p, ri_p, rg_p, cnt_p = _conv_layer(
        xp, None, *cv_w, cnt0, ts=512, seq_len=sp, **cv_common)
    hist = width - 1
    cache = jnp.pad(cache_conv[0].astype(F32), ((0, 0), (HIST_ROWS - hist, 0), (0, 0)))
    xs_, st_s, ri_s, rg_s, cnt_s = _conv_layer(
        xs_, cache, *cv_w, cnt_p, ts=512, seq_len=ss, **cv_common)
    conv_state_p = st_p[None, :, HIST_ROWS - hist:, :]
    conv_state_s = st_s[None, :, HIST_ROWS - hist:, :]

    tm = MOE_TILE
    n_assign = (np_ + ns_) * TOP_K
    n_tiles = (n_assign + n_experts * (tm - 1)) // tm
    counts = cnt_s[0, :n_experts].astype(jnp.int32)
    tiles_e = (counts + tm - 1) // tm
    tile_end = jnp.cumsum(tiles_e)
    offs = (tile_end - tiles_e) * tm
    n_used = tile_end[-1:]
    tile_idx = jnp.minimum(jnp.arange(n_tiles, dtype=jnp.int32), n_used[0] - 1)
    tile_expert = jnp.minimum(
        jnp.sum(tile_idx[:, None] >= tile_end[None, :], axis=1), n_experts - 1
    ).astype(jnp.int32)
    fill_starts = (jnp.maximum(tile_end, 1) - 1).astype(jnp.int32) * tm

    def dest_of(ri):
        e_idx = ri[:, 0:TOP_K]
        sel = e_idx[:, :, None] == jnp.arange(n_experts, dtype=jnp.int32)
        start = jnp.sum(jnp.where(sel, offs.astype(jnp.int32), 0), axis=-1)
        return (start + ri[:, TOP_K:2 * TOP_K]).reshape(-1).astype(jnp.int32)

    dest_p, dest_s = dest_of(ri_p), dest_of(ri_s)
    n_used = n_used.astype(jnp.int32)
    xsort = _dispatch(xp, dest_p, xs_, dest_s, fill_starts, n_used, tb=1024,
                      n_tiles=n_tiles, tile_rows=tm)
    ysort = _moe_grouped(
        xsort, tile_idx, tile_expert, n_used,
        moe_w_gate[0].astype(BF16), moe_w_up[0].astype(BF16),
        moe_w_down[0].astype(BF16), tm=tm, sub=MOE_F_SUB)
    fin = (_row(ln_g[1, 1]), _row(ln_b[1, 1]))
    yp = _combine(xp, rg_p, dest_p, ysort, *fin, tb=512, alpha=alpha)
    ys_ = _combine(xs_, rg_s, dest_s, ysort, *fin, tb=512, alpha=alpha)

    return (yp.reshape(bp, sp, d), ys_.reshape(bs, ss, d), gm_state_p, gm_state_s,
            conv_state_p, conv_state_s)
```
